```python
import jax, jax.numpy as jnp
from jax import lax
import numpy as np

D_MODEL = 1024
BATCH = 8
SEQ = 4096
DEPTH = 4

GRID_W = 64
CTX_LEN = 256
HEAD_DIM = 64
BLOCK = 128
ROPE_BASE = 10000.0
EPS = 1e-6
NEG_INF = -1e30
CONV_DIM = 512
CONV_WIDTH = 31
WIN_HEADS = 8
WIN_KV_HEADS = 2
WINDOW = 128
RET_HEADS = 4
RET_QK_DIM = 64
RET_V_DIM = 128
RET_CHUNK = 128
GLB_HEADS = 8
GLB_KV_HEADS = 2
N_BRANCH = 4
BRANCH_DIM = 512
D_FF = 2816
FFN_CONV_WIDTH = 3

IN_SIZES = (2 * CONV_DIM,
            WIN_HEADS * HEAD_DIM, WIN_KV_HEADS * HEAD_DIM, WIN_KV_HEADS * HEAD_DIM,
            RET_HEADS * RET_QK_DIM, RET_HEADS * RET_QK_DIM, RET_HEADS * RET_V_DIM, RET_HEADS * RET_V_DIM,
            GLB_HEADS * HEAD_DIM, GLB_KV_HEADS * HEAD_DIM, GLB_KV_HEADS * HEAD_DIM,
            N_BRANCH * D_MODEL)
IN_DIM = sum(IN_SIZES)
IN_OFFSETS = tuple(int(v) for v in np.cumsum(IN_SIZES)[:-1])

kernel_name = "hybrid_gated_branch_diffusion_trunk"


def rms_norm(x, g):
    xf = x.astype(jnp.float32)
    y = xf * lax.rsqrt(jnp.mean(xf * xf, axis=-1, keepdims=True) + EPS)
    return (y * g.astype(jnp.float32)).astype(x.dtype)


def layer_norm(x, g, b):
    xf = x.astype(jnp.float32)
    mu = jnp.mean(xf, axis=-1, keepdims=True)
    var = jnp.mean(jnp.square(xf - mu), axis=-1, keepdims=True)
    y = (xf - mu) * lax.rsqrt(var + EPS) * g.astype(jnp.float32) + b.astype(jnp.float32)
    return y.astype(x.dtype)


def modulate(x, g, shift, scale):
    return rms_norm(x, g) * (1 + scale) + shift


def heads(t, n_heads):
    b, n, _ = t.shape
    return t.reshape(b, n, n_heads, -1).transpose(0, 2, 1, 3)


def merge_heads(o):
    b, hk, g, n, dh = o.shape
    return o.reshape(b, hk * g, n, dh).transpose(0, 2, 1, 3).reshape(b, n, hk * g * dh)


def axial_rope_tables(n):
    rows = n // GRID_W
    row = jnp.repeat(jnp.arange(rows), GRID_W).astype(jnp.float32)
    col = jnp.tile(jnp.arange(GRID_W), rows).astype(jnp.float32)
    half = HEAD_DIM // 2
    inv = ROPE_BASE ** (-jnp.arange(0, half, 2, dtype=jnp.float32) / half)
    ang = jnp.concatenate([row[:, None] * inv, col[:, None] * inv], axis=-1)
    return jnp.cos(ang), jnp.sin(ang)


def apply_rope(x, cos, sin):
    xf = x.astype(jnp.float32)
    x1, x2 = jnp.split(xf, 2, axis=-1)
    return jnp.concatenate([x1 * cos - x2 * sin, x2 * cos + x1 * sin], axis=-1).astype(x.dtype)


def dwconv(x, w, b):
    k = w.shape[0]
    pad = (k - 1) // 2
    y = lax.conv_general_dilated(x, w[:, None, :].astype(x.dtype), (1,), [(pad, pad)],
                                 dimension_numbers=('NWC', 'WIO', 'NWC'),
                                 feature_group_count=x.shape[-1])
    return y + b.astype(x.dtype)


def attend(q, k, v, mask, sink):
    s = jnp.einsum('bhgqd,bhkd->bhgqk', q, k).astype(jnp.float32) * (HEAD_DIM ** -0.5)
    if mask is not None:
        s = jnp.where(mask, s, NEG_INF)
    if sink is not None:
        sk = jnp.broadcast_to(sink.astype(jnp.float32)[None, :, :, None, None], s.shape[:-1] + (1,))
        s = jnp.concatenate([s, sk], axis=-1)
    p = jax.nn.softmax(s, axis=-1)
    if sink is not None:
        p = p[..., :-1]
    return jnp.einsum('bhgqk,bhkd->bhgqd', p.astype(v.dtype), v)


def conv_branch(u, dw, dw_b, ln_g, ln_b):
    a, g = jnp.split(u, 2, axis=-1)
    z = a * jax.nn.sigmoid(g)
    z = dwconv(z, dw, dw_b)
    return jax.nn.silu(layer_norm(z, ln_g, ln_b))


def window_branch(q, k, v, qc, kc, vc, sink, cos, sin, with_ctx):
    b, s, _ = q.shape
    grp = WIN_HEADS // WIN_KV_HEADS
    q = apply_rope(heads(q, WIN_HEADS), cos, sin).reshape(b, WIN_KV_HEADS, grp, s, HEAD_DIM)
    k = apply_rope(heads(k, WIN_KV_HEADS), cos, sin)
    v = heads(v, WIN_KV_HEADS)
    kc = heads(kc, WIN_KV_HEADS)
    vc = heads(vc, WIN_KV_HEADS)
    sink_g = sink.reshape(WIN_KV_HEADS, grp)
    pad = ((0, 0), (0, 0), (BLOCK, BLOCK), (0, 0))
    kp, vp = jnp.pad(k, pad), jnp.pad(v, pad)
    qi = jnp.arange(BLOCK)
    kj = jnp.arange(3 * BLOCK)
    ctx_mask = jnp.ones((BLOCK, kc.shape[2]), dtype=bool)

    def block(n):
        qb = lax.dynamic_slice_in_dim(q, n * BLOCK, BLOCK, axis=3)
        kb = lax.dynamic_slice_in_dim(kp, n * BLOCK, 3 * BLOCK, axis=2)
        vb = lax.dynamic_slice_in_dim(vp, n * BLOCK, 3 * BLOCK, axis=2)
        ipos = n * BLOCK + qi
        jpos = (n - 1) * BLOCK + kj
        band = (jnp.abs(ipos[:, None] - jpos[None, :]) <= WINDOW) & (jpos >= 0)[None, :] & (jpos < s)[None, :]
        mask = jnp.concatenate([band, ctx_mask], axis=-1)
        return attend(qb, jnp.concatenate([kb, kc], axis=2), jnp.concatenate([vb, vc], axis=2), mask, sink_g)

    o = lax.map(block, jnp.arange(s // BLOCK))
    o = o.transpose(1, 2, 3, 0, 4, 5).reshape(b, WIN_KV_HEADS, grp, s, HEAD_DIM)
    y = merge_heads(o)
    if not with_ctx:
        return y, None
    qch = heads(qc, WIN_HEADS).reshape(b, WIN_KV_HEADS, grp, qc.shape[1], HEAD_DIM)
    return y, merge_heads(attend(qch, kc, vc, None, sink_g))


def global_branch(q, k, v, qc, kc, vc, qn_g, kn_g, cos, sin, with_ctx):
    b, s, _ = q.shape
    grp = GLB_HEADS // GLB_KV_HEADS
    q = apply_rope(rms_norm(heads(q, GLB_HEADS), qn_g), cos, sin).reshape(b, GLB_KV_HEADS, grp, s, HEAD_DIM)
    k = apply_rope(rms_norm(heads(k, GLB_KV_HEADS), kn_g), cos, sin)
    v = heads(v, GLB_KV_HEADS)
    kc = rms_norm(heads(kc, GLB_KV_HEADS), kn_g)
    vc = heads(vc, GLB_KV_HEADS)
    kall = jnp.concatenate([k, kc], axis=2)
    vall = jnp.concatenate([v, vc], axis=2)

    def block(n):
        qb = lax.dynamic_slice_in_dim(q, n * BLOCK, BLOCK, axis=3)
        return attend(qb, kall, vall, None, None)

    o = lax.map(block, jnp.arange(s // BLOCK))
    o = o.transpose(1, 2, 3, 0, 4, 5).reshape(b, GLB_KV_HEADS, grp, s, HEAD_DIM)
    y = merge_heads(o)
    if not with_ctx:
        return y, None
    qch = rms_norm(heads(qc, GLB_HEADS), qn_g).reshape(b, GLB_KV_HEADS, grp, qc.shape[1], HEAD_DIM)
    return y, merge_heads(attend(qch, kc, vc, None, None))


def retention_chunks(q, k, v, gamma, s0):
    b, h, n, dk = q.shape
    dv = v.shape[-1]
    cl = RET_CHUNK
    nc = n // cl
    qc = q.reshape(b, h, nc, cl, dk)
    kc = k.reshape(b, h, nc, cl, dk)
    vc = v.reshape(b, h, nc, cl, dv)
    lg = jnp.log(gamma)
    pos = jnp.arange(cl, dtype=jnp.float32)
    diff = pos[:, None] - pos[None, :]
    decay_in = jnp.where(diff >= 0, jnp.exp(lg[:, None, None] * jnp.maximum(diff, 0.0)), 0.0)
    sc = jnp.einsum('bhnid,bhnjd->bhnij', qc, kc) * decay_in[None, :, None]
    inner = jnp.einsum('bhnij,bhnje->bhnie', sc, vc)
    k_decay = jnp.exp(lg[:, None] * (cl - 1 - pos))
    q_decay = jnp.exp(lg[:, None] * (pos + 1))
    chunk_decay = jnp.exp(lg * cl)
    u = jnp.einsum('bhncd,bhnce->bhnde', kc * k_decay[None, :, None, :, None], vc)

    def step(state, u_c):
        return chunk_decay[None, :, None, None] * state + u_c, state

    s_fin, s_prev = lax.scan(step, s0, jnp.moveaxis(u, 2, 0))
    s_prev = jnp.moveaxis(s_prev, 0, 2)
    cross = jnp.einsum('bhncd,bhnde->bhnce', qc * q_decay[None, :, None, :, None], s_prev)
    return (inner + cross).reshape(b, h, n, dv), s_fin


def retention_out(o, g, gn_g):
    mu = jnp.mean(o, axis=-1, keepdims=True)
    var = jnp.mean(jnp.square(o - mu), axis=-1, keepdims=True)
    o = (o - mu) * lax.rsqrt(var + EPS)
    b, h, n, dv = o.shape
    o = o.transpose(0, 2, 1, 3).reshape(b, n, h * dv) * gn_g.astype(jnp.float32)
    return (jax.nn.silu(g.astype(jnp.float32)) * o).astype(g.dtype)


def retention_branch(q, k, v, g, qc, kc, vc, gc, decay_logit, gn_g, with_ctx):
    scale = RET_QK_DIM ** -0.5
    prep = lambda t: heads(t, RET_HEADS).astype(jnp.float32)
    flip = lambda t: jnp.flip(t, axis=2)
    q, k, v = prep(q), prep(k) * scale, prep(v)
    qc, kc, vc = prep(qc), prep(kc) * scale, prep(vc)
    gam = jax.nn.sigmoid(decay_logit.astype(jnp.float32))
    s0 = jnp.zeros((q.shape[0], RET_HEADS, RET_QK_DIM, RET_V_DIM), jnp.float32)
    oc_f, sc_f = retention_chunks(qc, kc, vc, gam[0], s0)
    oc_b, sc_b = retention_chunks(flip(qc), flip(kc), flip(vc), gam[1], s0)
    o_f, _ = retention_chunks(q, k, v, gam[0], sc_f)
    o_b, _ = retention_chunks(flip(q), flip(k), flip(v), gam[1], sc_b)
    y = retention_out(o_f + flip(o_b), g, gn_g)
    if not with_ctx:
        return y, None
    return y, retention_out(oc_f + flip(oc_b), gc, gn_g)


def merge_branches(branches, gate_logits, w_br, w_o):
    gl = jnp.split(gate_logits, N_BRANCH, axis=-1)
    acc = jax.nn.sigmoid(gl[0]) * (branches[0] @ w_br[0])
    for i in range(1, N_BRANCH):
        acc = acc + jax.nn.sigmoid(gl[i]) * (branches[i] @ w_br[i])
    return acc @ w_o


def token_mix(h, hc, cos, sin, w_in, a_dw, a_dw_b, a_ln_g, a_ln_b, b_sink, c_decay_logit, c_gn_g,
              d_qn_g, d_kn_g, w_br, w_o, with_ctx):
    (a_u, bq, bk, bv, cq, ck, cv, cg, dq, dk, dv, gates) = jnp.split(h @ w_in, IN_OFFSETS, axis=-1)
    (a_uc, bqc, bkc, bvc, cqc, ckc, cvc, cgc, dqc, dkc, dvc, gates_c) = jnp.split(hc @ w_in, IN_OFFSETS, axis=-1)
    ya = conv_branch(a_u, a_dw, a_dw_b, a_ln_g, a_ln_b)
    yb, yb_c = window_branch(bq, bk, bv, bqc, bkc, bvc, b_sink, cos, sin, with_ctx)
    yr, yr_c = retention_branch(cq, ck, cv, cg, cqc, ckc, cvc, cgc, c_decay_logit, c_gn_g, with_ctx)
    yd, yd_c = global_branch(dq, dk, dv, dqc, dkc, dvc, d_qn_g, d_kn_g, cos, sin, with_ctx)
    y = merge_branches([ya, yb, yr, yd], gates, w_br, w_o)
    if not with_ctx:
        return y, None
    ya_c = conv_branch(a_uc, a_dw, a_dw_b, a_ln_g, a_ln_b)
    return y, merge_branches([ya_c, yb_c, yr_c, yd_c], gates_c, w_br, w_o)


def conv_ffn(h, w_up, dw, dw_b, w_down):
    u = dwconv(h @ w_up, dw, dw_b)
    a, b = jnp.split(u, 2, axis=-1)
    return (jax.nn.silu(a) * b) @ w_down


def setup_inputs(seed: int = 0) -> dict:
    key = jax.random.key(seed)
    ks = jax.random.split(key, 24)
    f32 = jnp.float32
    nrm = lambda k, shape, fan: jax.random.normal(k, shape, f32) * (fan ** -0.5)
    small = lambda k, shape: 0.02 * jax.random.normal(k, shape, f32)
    gain = lambda k, shape: 1.0 + 0.05 * jax.random.normal(k, shape, f32)
    ret_init = jnp.log(2.0 ** (5.0 + jnp.arange(RET_HEADS, dtype=f32)) - 1.0)
    return {
        "x": jax.random.normal(ks[0], (BATCH, SEQ, D_MODEL), f32),
        "c": jax.random.normal(ks[1], (BATCH, D_MODEL), f32),
        "ctx": jax.random.normal(ks[2], (BATCH, CTX_LEN, D_MODEL), f32),
        "c_ctx": jax.random.normal(ks[3], (D_MODEL,), f32),
        "ada_w": 0.5 * nrm(ks[4], (DEPTH, D_MODEL, 6 * D_MODEL), D_MODEL),
        "ada_b": small(ks[5], (DEPTH, 6 * D_MODEL)),
        "norm_g": gain(ks[6], (DEPTH, 4, D_MODEL)),
        "w_in": nrm(ks[7], (DEPTH, D_MODEL, IN_DIM), D_MODEL),
        "a_dw": nrm(ks[8], (DEPTH, CONV_WIDTH, CONV_DIM), CONV_WIDTH),
        "a_dw_b": small(ks[9], (DEPTH, CONV_DIM)),
        "a_ln_g": gain(ks[10], (DEPTH, CONV_DIM)),
        "a_ln_b": small(ks[11], (DEPTH, CONV_DIM)),
        "b_sink": jax.random.normal(ks[12], (DEPTH, WIN_HEADS), f32),
        "c_decay_logit": ret_init[None, None, :] + 0.1 * jax.random.normal(ks[13], (DEPTH, 2, RET_HEADS), f32),
        "c_gn_g": gain(ks[14], (DEPTH, RET_HEADS * RET_V_DIM)),
        "d_qn_g": gain(ks[15], (DEPTH, HEAD_DIM)),
        "d_kn_g": gain(ks[16], (DEPTH, HEAD_DIM)),
        "w_br": nrm(ks[17], (DEPTH, N_BRANCH, BRANCH_DIM, D_MODEL), BRANCH_DIM),
        "w_o": nrm(ks[18], (DEPTH, D_MODEL, D_MODEL), D_MODEL),
        "f_up": nrm(ks[19], (DEPTH, D_MODEL, 2 * D_FF), D_MODEL),
        "f_dw": nrm(ks[20], (DEPTH, FFN_CONV_WIDTH, 2 * D_FF), FFN_CONV_WIDTH),
        "f_dw_b": small(ks[21], (DEPTH, 2 * D_FF)),
        "f_down": nrm(ks[22], (DEPTH, D_FF, D_MODEL), D_FF),
    }


def reference(x, c, ctx, c_ctx, ada_w, ada_b, norm_g, w_in, a_dw, a_dw_b, a_ln_g, a_ln_b, b_sink,
              c_decay_logit, c_gn_g, d_qn_g, d_kn_g, w_br, w_o, f_up, f_dw, f_dw_b, f_down):
    cos, sin = axial_rope_tables(x.shape[1])
    sc = jax.nn.silu(c)
    scc = jax.nn.silu(c_ctx)
    xc = ctx
    for l in range(DEPTH):
        with_ctx = l < DEPTH - 1
        m = (sc @ ada_w[l] + ada_b[l])[:, None, :]
        mc = scc @ ada_w[l] + ada_b[l]
        sh1, sc1, g1, sh2, sc2, g2 = jnp.split(m, 6, axis=-1)
        sh1c, sc1c, g1c, sh2c, sc2c, g2c = jnp.split(mc, 6, axis=-1)
        h = modulate(x, norm_g[l, 0], sh1, sc1)
        hc = modulate(xc, norm_g[l, 0], sh1c, sc1c)
        y, yc = token_mix(h, hc, cos, sin, w_in[l], a_dw[l], a_dw_b[l], a_ln_g[l], a_ln_b[l], b_sink[l],
                          c_decay_logit[l], c_gn_g[l], d_qn_g[l], d_kn_g[l], w_br[l], w_o[l], with_ctx)
        x = x + g1 * rms_norm(y, norm_g[l, 1])
        h = modulate(x, norm_g[l, 2], sh2, sc2)
        x = x + g2 * rms_norm(conv_ffn(h, f_up[l], f_dw[l], f_dw_b[l], f_down[l]), norm_g[l, 3])
        if with_ctx:
            xc = xc + g1c * rms_norm(yc, norm_g[l, 1])
            hc = modulate(xc, norm_g[l, 2], sh2c, sc2c)
            xc = xc + g2c * rms_norm(conv_ffn(hc, f_up[l], f_dw[l], f_dw_b[l], f_down[l]), norm_g[l, 3])
    return x
```

```python
import functools

import jax
import jax.numpy as jnp
import numpy as np
from jax import lax
from jax.experimental import pallas as pl
from jax.experimental.pallas import tpu as pltpu

D_MODEL = 1024
DEPTH = 4
GRID_W = 64
HEAD_DIM = 64
BLOCK = 128
ROPE_BASE = 10000.0
EPS = 1e-6
NEG_INF = -1e30
CONV_DIM = 512
CONV_WIDTH = 31
WIN_HEADS = 8
WIN_KV_HEADS = 2
WINDOW = 128
RET_HEADS = 4
RET_QK_DIM = 64
RET_V_DIM = 128
RET_CHUNK = 128
GLB_HEADS = 8
GLB_KV_HEADS = 2
N_BRANCH = 4
BRANCH_DIM = 512
D_FF = 2816
FFN_CONV_WIDTH = 3

IN_SIZES = (2 * CONV_DIM,
            WIN_HEADS * HEAD_DIM, WIN_KV_HEADS * HEAD_DIM, WIN_KV_HEADS * HEAD_DIM,
            RET_HEADS * RET_QK_DIM, RET_HEADS * RET_QK_DIM, RET_HEADS * RET_V_DIM, RET_HEADS * RET_V_DIM,
            GLB_HEADS * HEAD_DIM, GLB_KV_HEADS * HEAD_DIM, GLB_KV_HEADS * HEAD_DIM,
            N_BRANCH * D_MODEL)
IN_DIM = sum(IN_SIZES)
IN_OFF = tuple(int(v) for v in np.cumsum((0,) + IN_SIZES)[:-1])
(OFF_AU, OFF_BQ, OFF_BK, OFF_BV, OFF_CQ, OFF_CK, OFF_CV, OFF_CG,
 OFF_DQ, OFF_DK, OFF_DV, OFF_GATES) = IN_OFF

LANES = 128
SUBLANES = 8
MOD_ROWS = 16
VMEM_LIMIT = 56 * 1024 * 1024
F32 = jnp.float32
BF16 = jnp.bfloat16


def _sigmoid(v):
    return 1.0 / (1.0 + jnp.exp(-v))


def _silu(v):
    return v * _sigmoid(v)


def _cparams(*sem):
    return pltpu.CompilerParams(dimension_semantics=sem, vmem_limit_bytes=VMEM_LIMIT)


def _resident(shape):
    nd = len(shape)
    return pl.BlockSpec(shape, lambda *_: (0,) * nd, pipeline_mode=pl.Buffered(1))


def _ada_kernel(c_ref, w_ref, b_ref, o_ref):
    a = _silu(c_ref[...]).astype(BF16)
    w = w_ref[0].astype(BF16)
    o_ref[0] = jnp.dot(a, w, preferred_element_type=F32) + b_ref[0]


def _ada_call(cvec, ada_w, ada_b):
    tn = 1536
    n6 = 6 * D_MODEL
    return pl.pallas_call(
        _ada_kernel,
        grid=(DEPTH, n6 // tn),
        in_specs=[
            pl.BlockSpec((MOD_ROWS, D_MODEL), lambda l, j: (0, 0)),
            pl.BlockSpec((1, D_MODEL, tn), lambda l, j: (l, 0, j)),
            pl.BlockSpec((1, 1, tn), lambda l, j: (l, 0, j)),
        ],
        out_specs=pl.BlockSpec((1, MOD_ROWS, tn), lambda l, j: (l, 0, j)),
        out_shape=jax.ShapeDtypeStruct((DEPTH, MOD_ROWS, n6), F32),
        compiler_params=_cparams("arbitrary", "arbitrary"),
        name="ada_mod",
    )(cvec, ada_w, ada_b.reshape(DEPTH, 1, n6))


def _modulated(x, g, m, which):
    o = 3 * which * D_MODEL
    y = x * lax.rsqrt(jnp.mean(x * x, axis=-1, keepdims=True) + EPS) * g
    return y * (1.0 + m[:, o + D_MODEL:o + 2 * D_MODEL]) + m[:, o:o + D_MODEL]


def _rope128(v, cos, sa, sb):
    return v * cos + pltpu.roll(v, LANES - HEAD_DIM // 2, 1) * sa + pltpu.roll(v, HEAD_DIM // 2, 1) * sb


def _head_rms128(v, gain):
    lane = lax.broadcasted_iota(jnp.int32, (1, LANES), 1)
    lo = lane < HEAD_DIM
    sq = v * v
    s0 = jnp.sum(jnp.where(lo, sq, 0.0), axis=-1, keepdims=True)
    s1 = jnp.sum(jnp.where(lo, 0.0, sq), axis=-1, keepdims=True)
    ms = jnp.where(lo, s0, s1) * (1.0 / HEAD_DIM)
    return v * lax.rsqrt(ms + EPS) * gain


def _proj_kernel(x_ref, mod_ref, g_ref, w_ref, cos_ref, sa_ref, sb_ref, qn_ref, kn_ref,
                 z_ref, bq_ref, bk_ref, bv_ref, cq_ref, ck_ref, cv_ref, cg_ref,
                 dq_ref, dk_ref, dv_ref, gt_ref):
    h = _modulated(x_ref[...], g_ref[...], mod_ref[0], 0).astype(BF16)

    def proj(off, width):
        return jnp.dot(h, w_ref[:, off:off + width], preferred_element_type=F32)

    u = proj(OFF_AU, 2 * CONV_DIM)
    z_ref[...] = u[:, :CONV_DIM] * _sigmoid(u[:, CONV_DIM:])

    cos, sa, sb = cos_ref[...], sa_ref[...], sb_ref[...]
    scale = HEAD_DIM ** -0.5

    def roped(off, width, out_ref, gain_ref, mul):
        v = proj(off, width)
        for c in range(width // LANES):
            s = v[:, c * LANES:(c + 1) * LANES]
            if gain_ref is not None:
                s = _head_rms128(s, gain_ref[...])
            s = _rope128(s, cos, sa, sb)
            out_ref[:, c * LANES:(c + 1) * LANES] = (s * mul).astype(out_ref.dtype)

    roped(OFF_BQ, WIN_HEADS * HEAD_DIM, bq_ref, None, scale)
    roped(OFF_BK, WIN_KV_HEADS * HEAD_DIM, bk_ref, None, 1.0)
    bv_ref[...] = proj(OFF_BV, WIN_KV_HEADS * HEAD_DIM).astype(BF16)

    cq_ref[...] = proj(OFF_CQ, RET_HEADS * RET_QK_DIM).astype(BF16)
    ck_ref[...] = (proj(OFF_CK, RET_HEADS * RET_QK_DIM) * (RET_QK_DIM ** -0.5)).astype(BF16)
    cv_ref[...] = proj(OFF_CV, RET_HEADS * RET_V_DIM).astype(BF16)
    cg_ref[...] = proj(OFF_CG, RET_HEADS * RET_V_DIM)

    roped(OFF_DQ, GLB_HEADS * HEAD_DIM, dq_ref, qn_ref, scale)
    roped(OFF_DK, GLB_KV_HEADS * HEAD_DIM, dk_ref, kn_ref, 1.0)
    dv_ref[...] = proj(OFF_DV, GLB_KV_HEADS * HEAD_DIM).astype(BF16)

    for i in range(N_BRANCH):
        gl = proj(OFF_GATES + i * D_MODEL, D_MODEL)
        gt_ref[:, i * D_MODEL:(i + 1) * D_MODEL] = _sigmoid(gl).astype(BF16)


def _proj_call(dims, xall, mod, g, w_in, cos_t, sa_t, sb_t, qn2, kn2, tm):
    B, S, Lc, R = dims
    nlat = B * S // tm
    n_rope_lat = S // tm

    def row(i):
        return (i, 0)

    def mod_idx(i):
        return (jnp.minimum(i * tm // S, B), 0, 0)

    def rope_idx(i):
        return (jnp.where(i < nlat, i % n_rope_lat, n_rope_lat), 0)

    widths = [(CONV_DIM, F32), (512, BF16), (128, BF16), (128, BF16), (256, BF16), (256, BF16),
              (512, BF16), (512, F32), (512, BF16), (128, BF16), (128, BF16), (4 * D_MODEL, BF16)]
    return pl.pallas_call(
        _proj_kernel,
        grid=(R // tm,),
        in_specs=[
            pl.BlockSpec((tm, D_MODEL), row),
            pl.BlockSpec((1, 1, 6 * D_MODEL), mod_idx),
            pl.BlockSpec((1, D_MODEL), lambda i: (0, 0)),
            _resident((D_MODEL, IN_DIM)),
            pl.BlockSpec((tm, LANES), rope_idx),
            pl.BlockSpec((tm, LANES), rope_idx),
            pl.BlockSpec((tm, LANES), rope_idx),
            pl.BlockSpec((1, LANES), lambda i: (0, 0)),
            pl.BlockSpec((1, LANES), lambda i: (0, 0)),
        ],
        out_specs=[pl.BlockSpec((tm, w), row) for w, _ in widths],
        out_shape=[jax.ShapeDtypeStruct((R, w), dt) for w, dt in widths],
        compiler_params=_cparams("arbitrary"),
        name="proj",
    )(xall, mod, g, w_in, cos_t, sa_t, sb_t, qn2, kn2)


CONV_HALO = 16
CONV_ROWS = 32


def _seq_edges(i, tile, n_lat_rows, S, Lc):
    r0 = i * tile
    in_lat = r0 < n_lat_rows
    pos = jnp.where(in_lat, r0 % S, (r0 - n_lat_rows) % Lc)
    length = jnp.where(in_lat, S, Lc)
    return pos == 0, pos + tile == length


def _conv_kernel(dims, tc, z_ref, zp_ref, zn_ref, w_ref, b_ref, lg_ref, lb_ref, o_ref, buf):
    B, S, Lc, R = dims
    first, last = _seq_edges(pl.program_id(0), tc, B * S, S, Lc)
    buf[0:CONV_HALO, :] = jnp.where(first, 0.0, zp_ref[...])
    buf[CONV_HALO:CONV_HALO + tc, :] = z_ref[...]
    buf[CONV_HALO + tc:2 * CONV_HALO + tc, :] = jnp.where(last, 0.0, zn_ref[...])
    pad = (CONV_WIDTH - 1) // 2
    bias, lg, lb = b_ref[...], lg_ref[...], lb_ref[...]

    for c in range(tc // CONV_ROWS):
        r0 = c * CONV_ROWS
        acc = jnp.zeros((CONV_ROWS, CONV_DIM), F32) + bias
        for k in range(CONV_WIDTH):
            off = r0 + CONV_HALO - pad + k
            acc = acc + w_ref[k:k + 1, :] * buf[off:off + CONV_ROWS, :]
        mu = jnp.mean(acc, axis=-1, keepdims=True)
        xc = acc - mu
        var = jnp.mean(xc * xc, axis=-1, keepdims=True)
        y = xc * lax.rsqrt(var + EPS) * lg + lb
        o_ref[r0:r0 + CONV_ROWS, :] = _silu(y).astype(o_ref.dtype)


def _conv_call(dims, z, a_dw, a_dw_b, a_ln_g, a_ln_b, tc):
    B, S, Lc, R = dims
    hb = tc // CONV_HALO
    nhb = R // CONV_HALO
    return pl.pallas_call(
        functools.partial(_conv_kernel, dims, tc),
        grid=(R // tc,),
        in_specs=[
            pl.BlockSpec((tc, CONV_DIM), lambda i: (i, 0)),
            pl.BlockSpec((CONV_HALO, CONV_DIM), lambda i: (jnp.maximum(i * hb - 1, 0), 0)),
            pl.BlockSpec((CONV_HALO, CONV_DIM), lambda i: (jnp.minimum((i + 1) * hb, nhb - 1), 0)),
            pl.BlockSpec((CONV_WIDTH, CONV_DIM), lambda i: (0, 0)),
            pl.BlockSpec((1, CONV_DIM), lambda i: (0, 0)),
            pl.BlockSpec((1, CONV_DIM), lambda i: (0, 0)),
            pl.BlockSpec((1, CONV_DIM), lambda i: (0, 0)),
        ],
        out_specs=pl.BlockSpec((tc, CONV_DIM), lambda i: (i, 0)),
        out_shape=jax.ShapeDtypeStruct((R, CONV_DIM), BF16),
        scratch_shapes=[pltpu.VMEM((tc + 2 * CONV_HALO, CONV_DIM), F32)],
        compiler_params=_cparams("arbitrary"),
        name="conv_branch",
    )(z, z, z, a_dw, a_dw_b.reshape(1, -1), a_ln_g.reshape(1, -1), a_ln_b.reshape(1, -1))


def _window_kernel(dims, q_ref, kp_ref, kc_ref, kn_ref, kx_ref, vp_ref, vc_ref, vn_ref, vx_ref,
                   sink_ref, o_ref):
    B, S, Lc, R = dims
    nqb = S // BLOCK
    j = pl.program_id(1)
    is_lat = j < nqb
    grp = WIN_HEADS // WIN_KV_HEADS
    nk = 3 * BLOCK + Lc

    row = lax.broadcasted_iota(jnp.int32, (BLOCK, nk), 0)
    col = lax.broadcasted_iota(jnp.int32, (BLOCK, nk), 1)
    d = col - row
    lo = jnp.where(j >= 1, 0, BLOCK)
    hi = jnp.where(is_lat, jnp.where(j <= nqb - 2, 3 * BLOCK, 2 * BLOCK), 0)
    band = (d >= BLOCK - WINDOW) & (d <= BLOCK + WINDOW) & (col >= lo) & (col < hi)
    mask = band | (col >= 3 * BLOCK)
    mask = jnp.concatenate([mask] * grp, axis=0)

    q = q_ref[...]
    outs = []
    for kvh in range(WIN_KV_HEADS):
        ls = slice(kvh * HEAD_DIM, (kvh + 1) * HEAD_DIM)
        kk = jnp.concatenate([kp_ref[:, ls], kc_ref[:, ls], kn_ref[:, ls], kx_ref[:, ls]], axis=0)
        vv = jnp.concatenate([vp_ref[:, ls], vc_ref[:, ls], vn_ref[:, ls], vx_ref[:, ls]], axis=0)
        heads = [kvh * grp + g for g in range(grp)]
        qs = jnp.concatenate([q[:, h * HEAD_DIM:(h + 1) * HEAD_DIM] for h in heads], axis=0)
        sk = jnp.concatenate([jnp.full((BLOCK, 1), sink_ref[h], F32) for h in heads], axis=0)
        s = lax.dot_general(qs, kk, (((1,), (1,)), ((), ())), preferred_element_type=F32)
        s = jnp.where(mask, s, NEG_INF)
        m = jnp.maximum(jnp.max(s, axis=-1, keepdims=True), sk)
        p = jnp.exp(s - m)
        l = jnp.sum(p, axis=-1, keepdims=True) + jnp.exp(sk - m)
        o = jnp.dot(p.astype(BF16), vv, preferred_element_type=F32) * (1.0 / l)
        outs.extend(o[g * BLOCK:(g + 1) * BLOCK] for g in range(grp))
    o_ref[...] = jnp.concatenate(outs, axis=-1).astype(o_ref.dtype)


def _window_call(dims, bq, bk, bv, sink):
    B, S, Lc, R = dims
    nqb, ncb = S // BLOCK, Lc // BLOCK
    lat_blocks = B * nqb
    kvw = WIN_KV_HEADS * HEAD_DIM

    def q_idx(b, j):
        return (jnp.where(j < nqb, b * nqb + j, lat_blocks + b * ncb + (j - nqb)), 0)

    def k_idx(delta):
        return lambda b, j: (b * nqb + jnp.clip(j + delta, 0, nqb - 1), 0)

    def x_idx(b, j):
        return (B * S // Lc + b, 0)

    kspec = [pl.BlockSpec((BLOCK, kvw), k_idx(-1)), pl.BlockSpec((BLOCK, kvw), k_idx(0)),
             pl.BlockSpec((BLOCK, kvw), k_idx(1)), pl.BlockSpec((Lc, kvw), x_idx)]
    return pl.pallas_call(
        functools.partial(_window_kernel, dims),
        grid=(B, nqb + ncb),
        in_specs=[pl.BlockSpec((BLOCK, WIN_HEADS * HEAD_DIM), q_idx)] + kspec + kspec
        + [pl.BlockSpec(memory_space=pltpu.SMEM)],
        out_specs=pl.BlockSpec((BLOCK, WIN_HEADS * HEAD_DIM), q_idx),
        out_shape=jax.ShapeDtypeStruct((R, WIN_HEADS * HEAD_DIM), BF16),
        compiler_params=_cparams("arbitrary", "arbitrary"),
        name="window_attn",
    )(bq, bk, bk, bk, bk, bv, bv, bv, bv, sink)


GLB_TK = 512


def _global_kernel(dims, q_ref, kl_ref, kx_ref, vl_ref, vx_ref, o_ref):
    B, S, Lc, R = dims
    nqb = S // BLOCK
    is_lat = pl.program_id(1) < nqb
    grp = GLB_HEADS // GLB_KV_HEADS
    rows = grp * BLOCK
    q = q_ref[...]
    outs = []
    for kvh in range(GLB_KV_HEADS):
        ls = slice(kvh * HEAD_DIM, (kvh + 1) * HEAD_DIM)
        qs = jnp.concatenate([q[:, (kvh * grp + g) * HEAD_DIM:(kvh * grp + g + 1) * HEAD_DIM]
                              for g in range(grp)], axis=0)

        def update(carry, kk, vv):
            m, l, acc = carry
            s = lax.dot_general(qs, kk, (((1,), (1,)), ((), ())), preferred_element_type=F32)
            m_new = jnp.maximum(m, jnp.max(s, axis=-1, keepdims=True))
            alpha = jnp.exp(m - m_new)
            p = jnp.exp(s - m_new)
            l = alpha * l + jnp.sum(p, axis=-1, keepdims=True)
            acc = alpha * acc + jnp.dot(p.astype(BF16), vv, preferred_element_type=F32)
            return m_new, l, acc

        init = (jnp.full((rows, 1), NEG_INF, F32), jnp.zeros((rows, 1), F32),
                jnp.zeros((rows, HEAD_DIM), F32))
        carry = update(init, kx_ref[:, ls], vx_ref[:, ls])

        def body(c, carry):
            r0 = pl.multiple_of(c * GLB_TK, GLB_TK)
            return update(carry, kl_ref[pl.ds(r0, GLB_TK), ls], vl_ref[pl.ds(r0, GLB_TK), ls])

        m, l, acc = lax.fori_loop(0, jnp.where(is_lat, S // GLB_TK, 0), body, carry)
        o = acc * (1.0 / l)
        outs.extend(o[g * BLOCK:(g + 1) * BLOCK] for g in range(grp))
    o_ref[...] = jnp.concatenate(outs, axis=-1).astype(o_ref.dtype)


def _global_call(dims, dq, dk, dv):
    B, S, Lc, R = dims
    nqb, ncb = S // BLOCK, Lc // BLOCK
    lat_blocks = B * nqb
    kvw = GLB_KV_HEADS * HEAD_DIM

    def q_idx(b, j):
        return (jnp.where(j < nqb, b * nqb + j, lat_blocks + b * ncb + (j - nqb)), 0)

    lat = pl.BlockSpec((S, kvw), lambda b, j: (b, 0))
    ctx = pl.BlockSpec((Lc, kvw), lambda b, j: (B * S // Lc + b, 0))
    return pl.pallas_call(
        functools.partial(_global_kernel, dims),
        grid=(B, nqb + ncb),
        in_specs=[pl.BlockSpec((BLOCK, GLB_HEADS * HEAD_DIM), q_idx), lat, ctx, lat, ctx],
        out_specs=pl.BlockSpec((BLOCK, GLB_HEADS * HEAD_DIM), q_idx),
        out_shape=jax.ShapeDtypeStruct((R, GLB_HEADS * HEAD_DIM), BF16),
        compiler_params=_cparams("arbitrary", "arbitrary"),
        name="global_attn",
    )(dq, dk, dk, dv, dv)


RET_HPS = 2


def _ret_kernel(dims, ql_ref, qx_ref, kl_ref, kx_ref, vl_ref, vx_ref, gl_ref, gx_ref, lg_ref, gn_ref,
                ol_ref, ox_ref, fl_buf, fx_buf):
    B, S, Lc, R = dims
    C = RET_CHUNK
    hp = pl.program_id(1)
    pos_r = lax.broadcasted_iota(jnp.int32, (C, C), 0).astype(F32)
    pos_c = lax.broadcasted_iota(jnp.int32, (C, C), 1).astype(F32)
    diff = pos_r - pos_c
    col_pos = lax.broadcasted_iota(jnp.int32, (C, 1), 0).astype(F32)

    tabs = {}
    for dr in range(2):
        for hh in range(RET_HPS):
            lg = lg_ref[dr, hp * RET_HPS + hh]
            dd = diff if dr == 0 else -diff
            dmat = jnp.where(dd >= 0, jnp.exp(lg * jnp.maximum(dd, 0.0)), 0.0)
            p_eff = col_pos if dr == 0 else (C - 1.0) - col_pos
            kdec = jnp.exp(lg * ((C - 1.0) - p_eff))
            qdec = jnp.exp(lg * (p_eff + 1.0))
            cdec = jnp.exp(jnp.zeros((1, RET_V_DIM), F32) + lg * float(C))
            tabs[dr, hh] = (dmat, kdec, qdec, cdec)

    def step(q_ref, k_ref, v_ref, r0, dr, hh, state):
        dmat, kdec, qdec, cdec = tabs[dr, hh]
        qs = slice(hh * RET_QK_DIM, (hh + 1) * RET_QK_DIM)
        vs = slice(hh * RET_V_DIM, (hh + 1) * RET_V_DIM)
        qc = q_ref[pl.ds(r0, C), qs]
        kc = k_ref[pl.ds(r0, C), qs]
        vc = v_ref[pl.ds(r0, C), vs]
        sc = lax.dot_general(qc, kc, (((1,), (1,)), ((), ())), preferred_element_type=F32) * dmat
        inner = jnp.dot(sc.astype(BF16), vc, preferred_element_type=F32)
        qd = (qc.astype(F32) * qdec).astype(BF16)
        cross = jnp.dot(qd, state.astype(BF16), preferred_element_type=F32)
        kd = (kc.astype(F32) * kdec).astype(BF16)
        u = lax.dot_general(kd, vc, (((0,), (0,)), ((), ())), preferred_element_type=F32)
        return inner + cross, cdec * state + u

    def sweep(q_ref, k_ref, v_ref, f_buf, n_chunks, states):
        def body(i, st):
            st = list(st)
            rf = pl.multiple_of(i * C, C)
            rb = pl.multiple_of((n_chunks - 1 - i) * C, C)
            for hh in range(RET_HPS):
                vs = slice(hh * RET_V_DIM, (hh + 1) * RET_V_DIM)
                of, st[hh] = step(q_ref, k_ref, v_ref, rf, 0, hh, st[hh])
                f_buf[0, pl.ds(rf, C), vs] = of
                ob, st[RET_HPS + hh] = step(q_ref, k_ref, v_ref, rb, 1, hh, st[RET_HPS + hh])
                f_buf[1, pl.ds(rb, C), vs] = ob
            return tuple(st)
        return lax.fori_loop(0, n_chunks, body, states)

    def finish(f_buf, g_ref, o_ref, n_chunks):
        def body(i, carry):
            r0 = pl.multiple_of(i * C, C)
            for hh in range(RET_HPS):
                vs = slice(hh * RET_V_DIM, (hh + 1) * RET_V_DIM)
                o = f_buf[0, pl.ds(r0, C), vs] + f_buf[1, pl.ds(r0, C), vs]
                mu = jnp.mean(o, axis=-1, keepdims=True)
                xc = o - mu
                var = jnp.mean(xc * xc, axis=-1, keepdims=True)
                y = xc * lax.rsqrt(var + EPS) * gn_ref[:, vs]
                o_ref[pl.ds(r0, C), vs] = (_silu(g_ref[pl.ds(r0, C), vs]) * y).astype(o_ref.dtype)
            return carry
        lax.fori_loop(0, n_chunks, body, 0)

    zero = jnp.zeros((RET_QK_DIM, RET_V_DIM), F32)
    states = sweep(qx_ref, kx_ref, vx_ref, fx_buf, Lc // C, (zero,) * (2 * RET_HPS))
    sweep(ql_ref, kl_ref, vl_ref, fl_buf, S // C, states)
    finish(fx_buf, gx_ref, ox_ref, Lc // C)
    finish(fl_buf, gl_ref, ol_ref, S // C)


def _ret_call(dims, cq, ck, cv, cg, log_gamma, gn_g):
    B, S, Lc, R = dims
    qw = RET_HPS * RET_QK_DIM
    vw = RET_HPS * RET_V_DIM
    nhp = RET_HEADS // RET_HPS
    xoff = B * S // Lc

    def lat(w):
        return pl.BlockSpec((S, w), lambda b, h: (b, h))

    def ctx(w):
        return pl.BlockSpec((Lc, w), lambda b, h: (xoff + b, h))

    return pl.pallas_call(
        functools.partial(_ret_kernel, dims),
        grid=(B, nhp),
        in_specs=[lat(qw), ctx(qw), lat(qw), ctx(qw), lat(vw), ctx(vw), lat(vw), ctx(vw),
                  pl.BlockSpec(memory_space=pltpu.SMEM),
                  pl.BlockSpec((1, vw), lambda b, h: (0, h))],
        out_specs=[pl.BlockSpec((S, vw), lambda b, h: (b, h)),
                   pl.BlockSpec((Lc, vw), lambda b, h: (b, h))],
        out_shape=[jax.ShapeDtypeStruct((B * S, RET_HEADS * RET_V_DIM), BF16),
                   jax.ShapeDtypeStruct((B * Lc, RET_HEADS * RET_V_DIM), BF16)],
        scratch_shapes=[pltpu.VMEM((2, S, vw), F32), pltpu.VMEM((2, Lc, vw), F32)],
        compiler_params=_cparams("arbitrary", "arbitrary"),
        name="retention",
    )(cq, cq, ck, ck, cv, cv, cg, cg, log_gamma, gn_g.reshape(1, -1))


def _merge_kernel(ya_ref, yb_ref, yr_ref, yd_ref, gt_ref, x_ref, mod_ref, g_ref, wbr_ref, wo_ref, o_ref):
    acc = None
    for i, br in enumerate((ya_ref, yb_ref, yr_ref, yd_ref)):
        t = gt_ref[:, i * D_MODEL:(i + 1) * D_MODEL].astype(F32) * jnp.dot(
            br[...], wbr_ref[i], preferred_element_type=F32)
        acc = t if acc is None else acc + t
    y = jnp.dot(acc.astype(BF16), wo_ref[...], preferred_element_type=F32)
    r = y * lax.rsqrt(jnp.mean(y * y, axis=-1, keepdims=True) + EPS) * g_ref[...]
    gate = mod_ref[0][:, 2 * D_MODEL:3 * D_MODEL]
    o_ref[...] = x_ref[...] + gate * r


def _merge_call(dims, ya, yb, yr, yd, gates, xall, mod, g, w_br, w_o, tm):
    B, S, Lc, R = dims

    def row(i):
        return (i, 0)

    br = pl.BlockSpec((tm, BRANCH_DIM), row)
    return pl.pallas_call(
        _merge_kernel,
        grid=(R // tm,),
        in_specs=[br, br, br, br,
                  pl.BlockSpec((tm, N_BRANCH * D_MODEL), row),
                  pl.BlockSpec((tm, D_MODEL), row),
                  pl.BlockSpec((1, 1, 6 * D_MODEL), lambda i: (jnp.minimum(i * tm // S, B), 0, 0)),
                  pl.BlockSpec((1, D_MODEL), lambda i: (0, 0)),
                  _resident((N_BRANCH, BRANCH_DIM, D_MODEL)),
                  _resident((D_MODEL, D_MODEL))],
        out_specs=pl.BlockSpec((tm, D_MODEL), row),
        out_shape=jax.ShapeDtypeStruct((R, D_MODEL), F32),
        compiler_params=_cparams("arbitrary"),
        name="merge",
    )(ya, yb, yr, yd, gates, xall, mod, g, w_br, w_o)


FFN_HALO = 8
FFN_CN = 256


def _ffn_kernel(dims, tm, x_ref, xp_ref, xn_ref, mod_ref, g2_ref, g3_ref, wup_ref, dw_ref, db_ref, wdn_ref,
                o_ref, ua_buf, ub_buf, acc_buf):
    B, S, Lc, R = dims
    first, last = _seq_edges(pl.program_id(0), tm, B * S, S, Lc)
    m = mod_ref[0]
    g2 = g2_ref[...]
    x = x_ref[...]
    hp = jnp.where(first, 0.0, _modulated(xp_ref[...], g2, m, 1))
    hn = jnp.where(last, 0.0, _modulated(xn_ref[...], g2, m, 1))
    h = jnp.concatenate([hp, _modulated(x, g2, m, 1), hn], axis=0).astype(BF16)

    def tap(buf, col0, k):
        return dw_ref[k:k + 1, col0:col0 + FFN_CN] * buf[FFN_HALO - 1 + k:FFN_HALO - 1 + k + tm, :]

    for j in range(D_FF // FFN_CN):
        ca, cb = j * FFN_CN, D_FF + j * FFN_CN
        ua_buf[...] = jnp.dot(h, wup_ref[:, ca:ca + FFN_CN], preferred_element_type=F32)
        ub_buf[...] = jnp.dot(h, wup_ref[:, cb:cb + FFN_CN], preferred_element_type=F32)
        a = tap(ua_buf, ca, 0) + tap(ua_buf, ca, 1) + tap(ua_buf, ca, 2) + db_ref[:, ca:ca + FFN_CN]
        b = tap(ub_buf, cb, 0) + tap(ub_buf, cb, 1) + tap(ub_buf, cb, 2) + db_ref[:, cb:cb + FFN_CN]
        t = (_silu(a) * b).astype(BF16)
        part = jnp.dot(t, wdn_ref[ca:ca + FFN_CN, :], preferred_element_type=F32)
        if j == 0:
            acc_buf[...] = part
        else:
            acc_buf[...] += part

    y = acc_buf[...]
    r = y * lax.rsqrt(jnp.mean(y * y, axis=-1, keepdims=True) + EPS) * g3_ref[...]
    o_ref[...] = x + m[:, 5 * D_MODEL:6 * D_MODEL] * r


def _ffn_call(dims, xall, mod, g2, g3, f_up, f_dw, f_dw_b, f_down, tm):
    B, S, Lc, R = dims
    hb = tm // FFN_HALO
    nhb = R // FFN_HALO
    return pl.pallas_call(
        functools.partial(_ffn_kernel, dims, tm),
        grid=(R // tm,),
        in_specs=[
            pl.BlockSpec((tm, D_MODEL), lambda i: (i, 0)),
            pl.BlockSpec((FFN_HALO, D_MODEL), lambda i: (jnp.maximum(i * hb - 1, 0), 0)),
            pl.BlockSpec((FFN_HALO, D_MODEL), lambda i: (jnp.minimum((i + 1) * hb, nhb - 1), 0)),
            pl.BlockSpec((1, 1, 6 * D_MODEL), lambda i: (jnp.minimum(i * tm // S, B), 0, 0)),
            pl.BlockSpec((1, D_MODEL), lambda i: (0, 0)),
            pl.BlockSpec((1, D_MODEL), lambda i: (0, 0)),
            _resident((D_MODEL, 2 * D_FF)),
            pl.BlockSpec((FFN_CONV_WIDTH, 2 * D_FF), lambda i: (0, 0)),
            pl.BlockSpec((1, 2 * D_FF), lambda i: (0, 0)),
            _resident((D_FF, D_MODEL)),
        ],
        out_specs=pl.BlockSpec((tm, D_MODEL), lambda i: (i, 0)),
        out_shape=jax.ShapeDtypeStruct((R, D_MODEL), F32),
        scratch_shapes=[pltpu.VMEM((tm + 2 * FFN_HALO, FFN_CN), F32),
                        pltpu.VMEM((tm + 2 * FFN_HALO, FFN_CN), F32),
                        pltpu.VMEM((tm, D_MODEL), F32)],
        compiler_params=_cparams("arbitrary"),
        name="conv_ffn",
    )(xall, xall, xall, mod, g2, g3, f_up, f_dw, f_dw_b.reshape(1, -1), f_down)


def _rope_tables(S, pad_rows):
    rows = S // GRID_W
    row = jnp.repeat(jnp.arange(rows), GRID_W).astype(F32)
    col = jnp.tile(jnp.arange(GRID_W), rows).astype(F32)
    half = HEAD_DIM // 2
    inv = ROPE_BASE ** (-jnp.arange(0, half, 2, dtype=F32) / half)
    ang = jnp.concatenate([row[:, None] * inv, col[:, None] * inv], axis=-1)
    cos, sin = jnp.cos(ang), jnp.sin(ang)
    zero = jnp.zeros_like(sin)
    cos_h = jnp.concatenate([cos, cos], axis=-1)
    sa_h = jnp.concatenate([-sin, zero], axis=-1)
    sb_h = jnp.concatenate([zero, sin], axis=-1)
    reps = LANES // HEAD_DIM

    def slab(t, fill):
        t = jnp.tile(t, (1, reps))
        return jnp.concatenate([t, jnp.full((pad_rows, LANES), fill, F32)], axis=0)

    return slab(cos_h, 1.0), slab(sa_h, 0.0), slab(sb_h, 0.0)


def kernel(x, c, ctx, c_ctx, ada_w, ada_b, norm_g, w_in, a_dw, a_dw_b, a_ln_g, a_ln_b, b_sink,
           c_decay_logit, c_gn_g, d_qn_g, d_kn_g, w_br, w_o, f_up, f_dw, f_dw_b, f_down):
    B, S, _ = x.shape
    Lc = ctx.shape[1]
    R = B * (S + Lc)
    dims = (B, S, Lc, R)
    tm = 256
    assert B + 1 <= MOD_ROWS and S % tm == 0 and (B * Lc) % tm == 0 and Lc % BLOCK == 0 and S % Lc == 0

    cvec = jnp.concatenate([c, c_ctx[None, :], jnp.zeros((MOD_ROWS - B - 1, D_MODEL), F32)], axis=0)
    mods = _ada_call(cvec, ada_w, ada_b).reshape(DEPTH, MOD_ROWS, 1, 6 * D_MODEL)
    cos_t, sa_t, sb_t = _rope_tables(S, tm)
    reps = LANES // HEAD_DIM
    log_gamma = jnp.log(jax.nn.sigmoid(c_decay_logit.astype(F32)))

    xall = jnp.concatenate([x.reshape(B * S, D_MODEL), ctx.reshape(B * Lc, D_MODEL)], axis=0)
    for l in range(DEPTH):
        mod = mods[l]
        ng = norm_g[l].reshape(4, 1, D_MODEL)
        (z, bq, bk, bv, cq, ck, cv, cg, dq, dk, dv, gates) = _proj_call(
            dims, xall, mod, ng[0], w_in[l].astype(BF16), cos_t, sa_t, sb_t,
            jnp.tile(d_qn_g[l], reps)[None, :], jnp.tile(d_kn_g[l], reps)[None, :], tm)
        ya = _conv_call(dims, z, a_dw[l], a_dw_b[l], a_ln_g[l], a_ln_b[l], 256)
        yb = _window_call(dims, bq, bk, bv, b_sink[l])
        yr_l, yr_x = _ret_call(dims, cq, ck, cv, cg, log_gamma[l], c_gn_g[l])
        yr = jnp.concatenate([yr_l, yr_x], axis=0)
        yd = _global_call(dims, dq, dk, dv)
        xall = _merge_call(dims, ya, yb, yr, yd, gates, xall, mod, ng[1],
                           w_br[l].astype(BF16), w_o[l].astype(BF16), tm)
        xall = _ffn_call(dims, xall, mod, ng[2], ng[3], f_up[l].astype(BF16), f_dw[l], f_dw_b[l],
                         f_down[l].astype(BF16), tm)
    return xall[:B * S].reshape(B, S, D_MODEL)
```

```python
import functools

import jax
import jax.numpy as jnp
import numpy as np
from jax import lax
from jax.experimental import pallas as pl
from jax.experimental.pallas import tpu as pltpu

D_MODEL = 1024
DEPTH = 4
GRID_W = 64
HEAD_DIM = 64
BLOCK = 128
ROPE_BASE = 10000.0
EPS = 1e-6
NEG_INF = -1e30
CONV_DIM = 512
CONV_WIDTH = 31
WIN_HEADS = 8
WIN_KV_HEADS = 2
WINDOW = 128
RET_HEADS = 4
RET_QK_DIM = 64
RET_V_DIM = 128
RET_CHUNK = 128
GLB_HEADS = 8
GLB_KV_HEADS = 2
N_BRANCH = 4
BRANCH_DIM = 512
D_FF = 2816
FFN_CONV_WIDTH = 3

IN_SIZES = (2 * CONV_DIM,
            WIN_HEADS * HEAD_DIM, WIN_KV_HEADS * HEAD_DIM, WIN_KV_HEADS * HEAD_DIM,
            RET_HEADS * RET_QK_DIM, RET_HEADS * RET_QK_DIM, RET_HEADS * RET_V_DIM, RET_HEADS * RET_V_DIM,
            GLB_HEADS * HEAD_DIM, GLB_KV_HEADS * HEAD_DIM, GLB_KV_HEADS * HEAD_DIM,
            N_BRANCH * D_MODEL)
IN_DIM = sum(IN_SIZES)
IN_OFF = tuple(int(v) for v in np.cumsum((0,) + IN_SIZES)[:-1])
(OFF_AU, OFF_BQ, OFF_BK, OFF_BV, OFF_CQ, OFF_CK, OFF_CV, OFF_CG,
 OFF_DQ, OFF_DK, OFF_DV, OFF_GATES) = IN_OFF

LANES = 128
SUBLANES = 8
MOD_ROWS = 16
VMEM_LIMIT = 56 * 1024 * 1024
F32 = jnp.float32
BF16 = jnp.bfloat16
LOG2E = 1.4426950408889634


def _sigmoid(v):
    return 1.0 / (1.0 + jnp.exp(-v))


def _silu(v):
    return v * _sigmoid(v)


def _cparams(*sem):
    return pltpu.CompilerParams(dimension_semantics=sem, vmem_limit_bytes=VMEM_LIMIT)


def _resident(shape):
    nd = len(shape)
    return pl.BlockSpec(shape, lambda *_: (0,) * nd, pipeline_mode=pl.Buffered(1))


def _ada_kernel(c_ref, w_ref, b_ref, o_ref):
    a = _silu(c_ref[...]).astype(BF16)
    w = w_ref[0].astype(BF16)
    o_ref[0] = jnp.dot(a, w, preferred_element_type=F32) + b_ref[0]


def _ada_call(cvec, ada_w, ada_b):
    tn = 1536
    n6 = 6 * D_MODEL
    return pl.pallas_call(
        _ada_kernel,
        grid=(DEPTH, n6 // tn),
        in_specs=[
            pl.BlockSpec((MOD_ROWS, D_MODEL), lambda l, j: (0, 0)),
            pl.BlockSpec((1, D_MODEL, tn), lambda l, j: (l, 0, j)),
            pl.BlockSpec((1, 1, tn), lambda l, j: (l, 0, j)),
        ],
        out_specs=pl.BlockSpec((1, MOD_ROWS, tn), lambda l, j: (l, 0, j)),
        out_shape=jax.ShapeDtypeStruct((DEPTH, MOD_ROWS, n6), F32),
        compiler_params=_cparams("arbitrary", "arbitrary"),
        name="ada_mod",
    )(cvec, ada_w, ada_b.reshape(DEPTH, 1, n6))


def _modulated(x, g, m, which):
    o = 3 * which * D_MODEL
    y = x * lax.rsqrt(jnp.mean(x * x, axis=-1, keepdims=True) + EPS) * g
    return y * (1.0 + m[:, o + D_MODEL:o + 2 * D_MODEL]) + m[:, o:o + D_MODEL]


def _rope128(v, cos, sa, sb):
    return v * cos + pltpu.roll(v, LANES - HEAD_DIM // 2, 1) * sa + pltpu.roll(v, HEAD_DIM // 2, 1) * sb


def _head_rms128(v, gain):
    lane = lax.broadcasted_iota(jnp.int32, (1, LANES), 1)
    lo = lane < HEAD_DIM
    sq = v * v
    s0 = jnp.sum(jnp.where(lo, sq, 0.0), axis=-1, keepdims=True)
    s1 = jnp.sum(jnp.where(lo, 0.0, sq), axis=-1, keepdims=True)
    ms = jnp.where(lo, s0, s1) * (1.0 / HEAD_DIM)
    return v * lax.rsqrt(ms + EPS) * gain


def _proj_kernel(x_ref, mod_ref, g_ref, w_ref, cos_ref, sa_ref, sb_ref, qn_ref, kn_ref,
                 z_ref, bq_ref, bk_ref, bv_ref, cq_ref, ck_ref, cv_ref, cg_ref,
                 dq_ref, dk_ref, dv_ref, gt_ref):
    h = _modulated(x_ref[...], g_ref[...], mod_ref[0], 0).astype(BF16)

    def proj(off, width):
        return jnp.dot(h, w_ref[:, off:off + width], preferred_element_type=F32)

    u = proj(OFF_AU, 2 * CONV_DIM)
    z_ref[...] = u[:, :CONV_DIM] * _sigmoid(u[:, CONV_DIM:])

    cos, sa, sb = cos_ref[...], sa_ref[...], sb_ref[...]
    scale = HEAD_DIM ** -0.5

    def roped(off, width, out_ref, gain_ref, mul, feature_major=False):
        v = proj(off, width)
        for c in range(width // LANES):
            s = v[:, c * LANES:(c + 1) * LANES]
            if gain_ref is not None:
                s = _head_rms128(s, gain_ref[...])
            s = _rope128(s, cos, sa, sb) * mul
            if feature_major:
                out_ref[c * LANES:(c + 1) * LANES, :] = s.T.astype(out_ref.dtype)
            else:
                out_ref[:, c * LANES:(c + 1) * LANES] = s.astype(out_ref.dtype)

    roped(OFF_BQ, WIN_HEADS * HEAD_DIM, bq_ref, None, scale)
    roped(OFF_BK, WIN_KV_HEADS * HEAD_DIM, bk_ref, None, 1.0)
    bv_ref[...] = proj(OFF_BV, WIN_KV_HEADS * HEAD_DIM).astype(BF16)

    cq_ref[...] = proj(OFF_CQ, RET_HEADS * RET_QK_DIM).astype(BF16)
    ck_ref[...] = (proj(OFF_CK, RET_HEADS * RET_QK_DIM) * (RET_QK_DIM ** -0.5)).astype(BF16)
    cv_ref[...] = proj(OFF_CV, RET_HEADS * RET_V_DIM).astype(BF16)
    cg_ref[...] = proj(OFF_CG, RET_HEADS * RET_V_DIM)

    roped(OFF_DQ, GLB_HEADS * HEAD_DIM, dq_ref, qn_ref, scale * LOG2E, feature_major=True)
    roped(OFF_DK, GLB_KV_HEADS * HEAD_DIM, dk_ref, kn_ref, 1.0)
    dv_ref[...] = proj(OFF_DV, GLB_KV_HEADS * HEAD_DIM).astype(BF16).T

    for i in range(N_BRANCH):
        gl = proj(OFF_GATES + i * D_MODEL, D_MODEL)
        gt_ref[:, i * D_MODEL:(i + 1) * D_MODEL] = _sigmoid(gl).astype(BF16)


def _proj_call(dims, xall, mod, g, w_in, cos_t, sa_t, sb_t, qn2, kn2, tm):
    B, S, Lc, R = dims
    nlat = B * S // tm
    n_rope_lat = S // tm

    def row(i):
        return (i, 0)

    def mod_idx(i):
        return (jnp.minimum(i * tm // S, B), 0, 0)

    def rope_idx(i):
        return (jnp.where(i < nlat, i % n_rope_lat, n_rope_lat), 0)

    outs = [(CONV_DIM, F32, False), (512, BF16, False), (128, BF16, False), (128, BF16, False),
            (256, BF16, False), (256, BF16, False), (512, BF16, False), (512, F32, False),
            (512, BF16, True), (128, BF16, False), (128, BF16, True), (4 * D_MODEL, BF16, False)]
    return pl.pallas_call(
        _proj_kernel,
        grid=(R // tm,),
        in_specs=[
            pl.BlockSpec((tm, D_MODEL), row),
            pl.BlockSpec((1, 1, 6 * D_MODEL), mod_idx),
            pl.BlockSpec((1, D_MODEL), lambda i: (0, 0)),
            _resident((D_MODEL, IN_DIM)),
            pl.BlockSpec((tm, LANES), rope_idx),
            pl.BlockSpec((tm, LANES), rope_idx),
            pl.BlockSpec((tm, LANES), rope_idx),
            pl.BlockSpec((1, LANES), lambda i: (0, 0)),
            pl.BlockSpec((1, LANES), lambda i: (0, 0)),
        ],
        out_specs=[pl.BlockSpec((w, tm), lambda i: (0, i)) if fm else pl.BlockSpec((tm, w), row)
                   for w, _, fm in outs],
        out_shape=[jax.ShapeDtypeStruct((w, R) if fm else (R, w), dt) for w, dt, fm in outs],
        compiler_params=_cparams("arbitrary"),
        name="proj",
    )(xall, mod, g, w_in, cos_t, sa_t, sb_t, qn2, kn2)


CONV_HALO = 16
CONV_ROWS = 32


def _seq_edges(i, tile, n_lat_rows, S, Lc):
    r0 = i * tile
    in_lat = r0 < n_lat_rows
    pos = jnp.where(in_lat, r0 % S, (r0 - n_lat_rows) % Lc)
    length = jnp.where(in_lat, S, Lc)
    return pos == 0, pos + tile == length


def _conv_kernel(dims, tc, z_ref, zp_ref, zn_ref, w_ref, b_ref, lg_ref, lb_ref, o_ref, buf):
    B, S, Lc, R = dims
    first, last = _seq_edges(pl.program_id(0), tc, B * S, S, Lc)
    buf[0:CONV_HALO, :] = jnp.where(first, 0.0, zp_ref[...])
    buf[CONV_HALO:CONV_HALO + tc, :] = z_ref[...]
    buf[CONV_HALO + tc:2 * CONV_HALO + tc, :] = jnp.where(last, 0.0, zn_ref[...])
    pad = (CONV_WIDTH - 1) // 2
    bias, lg, lb = b_ref[...], lg_ref[...], lb_ref[...]

    for c in range(tc // CONV_ROWS):
        r0 = c * CONV_ROWS
        acc = jnp.zeros((CONV_ROWS, CONV_DIM), F32) + bias
        for k in range(CONV_WIDTH):
            off = r0 + CONV_HALO - pad + k
            acc = acc + w_ref[k:k + 1, :] * buf[off:off + CONV_ROWS, :]
        mu = jnp.mean(acc, axis=-1, keepdims=True)
        xc = acc - mu
        var = jnp.mean(xc * xc, axis=-1, keepdims=True)
        y = xc * lax.rsqrt(var + EPS) * lg + lb
        o_ref[r0:r0 + CONV_ROWS, :] = _silu(y).astype(o_ref.dtype)


def _conv_call(dims, z, a_dw, a_dw_b, a_ln_g, a_ln_b, tc):
    B, S, Lc, R = dims
    hb = tc // CONV_HALO
    nhb = R // CONV_HALO
    return pl.pallas_call(
        functools.partial(_conv_kernel, dims, tc),
        grid=(R // tc,),
        in_specs=[
            pl.BlockSpec((tc, CONV_DIM), lambda i: (i, 0)),
            pl.BlockSpec((CONV_HALO, CONV_DIM), lambda i: (jnp.maximum(i * hb - 1, 0), 0)),
            pl.BlockSpec((CONV_HALO, CONV_DIM), lambda i: (jnp.minimum((i + 1) * hb, nhb - 1), 0)),
            pl.BlockSpec((CONV_WIDTH, CONV_DIM), lambda i: (0, 0)),
            pl.BlockSpec((1, CONV_DIM), lambda i: (0, 0)),
            pl.BlockSpec((1, CONV_DIM), lambda i: (0, 0)),
            pl.BlockSpec((1, CONV_DIM), lambda i: (0, 0)),
        ],
        out_specs=pl.BlockSpec((tc, CONV_DIM), lambda i: (i, 0)),
        out_shape=jax.ShapeDtypeStruct((R, CONV_DIM), BF16),
        scratch_shapes=[pltpu.VMEM((tc + 2 * CONV_HALO, CONV_DIM), F32)],
        compiler_params=_cparams("arbitrary"),
        name="conv_branch",
    )(z, z, z, a_dw, a_dw_b.reshape(1, -1), a_ln_g.reshape(1, -1), a_ln_b.reshape(1, -1))


def _window_kernel(dims, q_ref, kp_ref, kc_ref, kn_ref, kx_ref, vp_ref, vc_ref, vn_ref, vx_ref,
                   sink_ref, o_ref):
    B, S, Lc, R = dims
    nqb = S // BLOCK
    j = pl.program_id(1)
    is_lat = j < nqb
    grp = WIN_HEADS // WIN_KV_HEADS
    nk = 3 * BLOCK + Lc

    row = lax.broadcasted_iota(jnp.int32, (BLOCK, nk), 0)
    col = lax.broadcasted_iota(jnp.int32, (BLOCK, nk), 1)
    d = col - row
    lo = jnp.where(j >= 1, 0, BLOCK)
    hi = jnp.where(is_lat, jnp.where(j <= nqb - 2, 3 * BLOCK, 2 * BLOCK), 0)
    band = (d >= BLOCK - WINDOW) & (d <= BLOCK + WINDOW) & (col >= lo) & (col < hi)
    mask = band | (col >= 3 * BLOCK)
    mask = jnp.concatenate([mask] * grp, axis=0)

    q = q_ref[...]
    outs = []
    for kvh in range(WIN_KV_HEADS):
        ls = slice(kvh * HEAD_DIM, (kvh + 1) * HEAD_DIM)
        kk = jnp.concatenate([kp_ref[:, ls], kc_ref[:, ls], kn_ref[:, ls], kx_ref[:, ls]], axis=0)
        vv = jnp.concatenate([vp_ref[:, ls], vc_ref[:, ls], vn_ref[:, ls], vx_ref[:, ls]], axis=0)
        heads = [kvh * grp + g for g in range(grp)]
        qs = jnp.concatenate([q[:, h * HEAD_DIM:(h + 1) * HEAD_DIM] for h in heads], axis=0)
        sk = jnp.concatenate([jnp.full((BLOCK, 1), sink_ref[h], F32) for h in heads], axis=0)
        s = lax.dot_general(qs, kk, (((1,), (1,)), ((), ())), preferred_element_type=F32)
        s = jnp.where(mask, s, NEG_INF)
        m = jnp.maximum(jnp.max(s, axis=-1, keepdims=True), sk)
        p = jnp.exp(s - m)
        l = jnp.sum(p, axis=-1, keepdims=True) + jnp.exp(sk - m)
        o = jnp.dot(p.astype(BF16), vv, preferred_element_type=F32) * (1.0 / l)
        outs.extend(o[g * BLOCK:(g + 1) * BLOCK] for g in range(grp))
    o_ref[...] = jnp.concatenate(outs, axis=-1).astype(o_ref.dtype)


def _window_call(dims, bq, bk, bv, sink):
    B, S, Lc, R = dims
    nqb, ncb = S // BLOCK, Lc // BLOCK
    lat_blocks = B * nqb
    kvw = WIN_KV_HEADS * HEAD_DIM

    def q_idx(b, j):
        return (jnp.where(j < nqb, b * nqb + j, lat_blocks + b * ncb + (j - nqb)), 0)

    def k_idx(delta):
        return lambda b, j: (b * nqb + jnp.clip(j + delta, 0, nqb - 1), 0)

    def x_idx(b, j):
        return (B * S // Lc + b, 0)

    kspec = [pl.BlockSpec((BLOCK, kvw), k_idx(-1)), pl.BlockSpec((BLOCK, kvw), k_idx(0)),
             pl.BlockSpec((BLOCK, kvw), k_idx(1)), pl.BlockSpec((Lc, kvw), x_idx)]
    return pl.pallas_call(
        functools.partial(_window_kernel, dims),
        grid=(B, nqb + ncb),
        in_specs=[pl.BlockSpec((BLOCK, WIN_HEADS * HEAD_DIM), q_idx)] + kspec + kspec
        + [pl.BlockSpec(memory_space=pltpu.SMEM)],
        out_specs=pl.BlockSpec((BLOCK, WIN_HEADS * HEAD_DIM), q_idx),
        out_shape=jax.ShapeDtypeStruct((R, WIN_HEADS * HEAD_DIM), BF16),
        compiler_params=_cparams("arbitrary", "arbitrary"),
        name="window_attn",
    )(bq, bk, bk, bk, bk, bv, bv, bv, bv, sink)


GLB_TK = 128
GLB_ONLINE_TK = 256
GLB_MAX_SAFE_BOUND = 40.0


def _global_kernel(dims, par_ref, qT_ref, kl_ref, kx_ref, vTl_ref, vTx_ref, o_ref):
    B, S, Lc, R = dims
    nqb = S // BLOCK
    is_lat = pl.program_id(1) < nqb
    grp = GLB_HEADS // GLB_KV_HEADS
    cols = grp * BLOCK
    shift = par_ref[0]
    qT = qT_ref[...]
    zeros = jnp.zeros((HEAD_DIM, cols), BF16)
    W = []
    for kvh in range(GLB_KV_HEADS):
        piece = jnp.concatenate([qT[(kvh * grp + g) * HEAD_DIM:(kvh * grp + g + 1) * HEAD_DIM, :]
                                 for g in range(grp)], axis=1)
        W.append(jnp.concatenate([piece, zeros] if kvh == 0 else [zeros, piece], axis=0))

    def scores(kk):
        return tuple(jnp.dot(kk, W[kvh], preferred_element_type=F32) for kvh in range(GLB_KV_HEADS))

    def v_rows(vT, kvh):
        return vT[kvh * HEAD_DIM:(kvh + 1) * HEAD_DIM, :]

    def bounded(chunks):
        ls = [jnp.zeros((1, cols), F32)] * GLB_KV_HEADS
        accs = [jnp.zeros((HEAD_DIM, cols), F32)] * GLB_KV_HEADS
        s_cur = scores(chunks[0][0][chunks[0][2]:chunks[0][2] + GLB_TK, :])
        for i, (k_ref, v_ref, r0) in enumerate(chunks):
            if i + 1 < len(chunks):
                nk, _, nr = chunks[i + 1]
                s_next = scores(nk[nr:nr + GLB_TK, :])
            vT = v_ref[:, r0:r0 + GLB_TK]
            for kvh in range(GLB_KV_HEADS):
                pT = jnp.exp2(s_cur[kvh] - shift)
                ls[kvh] = ls[kvh] + jnp.sum(pT, axis=0, keepdims=True)
                accs[kvh] = accs[kvh] + jnp.dot(v_rows(vT, kvh), pT.astype(BF16), preferred_element_type=F32)
            s_cur = s_next
        return tuple(acc * (1.0 / l) for l, acc in zip(ls, accs))

    def online():
        def update(carry, kk, vT):
            new = []
            for kvh, sT in enumerate(scores(kk)):
                m, l, acc = carry[kvh]
                m_new = jnp.maximum(m, jnp.max(sT, axis=0, keepdims=True))
                alpha = jnp.exp2(m - m_new)
                pT = jnp.exp2(sT - m_new)
                l = alpha * l + jnp.sum(pT, axis=0, keepdims=True)
                acc = alpha * acc + jnp.dot(v_rows(vT, kvh), pT.astype(BF16), preferred_element_type=F32)
                new.append((m_new, l, acc))
            return tuple(new)

        init = tuple((jnp.full((1, cols), NEG_INF, F32), jnp.zeros((1, cols), F32),
                      jnp.zeros((HEAD_DIM, cols), F32)) for _ in range(GLB_KV_HEADS))
        carry = init
        for r0 in range(0, Lc, GLB_ONLINE_TK):
            carry = update(carry, kx_ref[r0:r0 + GLB_ONLINE_TK, :], vTx_ref[:, r0:r0 + GLB_ONLINE_TK])

        def body(c, carry):
            r0 = pl.multiple_of(c * GLB_ONLINE_TK, GLB_ONLINE_TK)
            return update(carry, kl_ref[pl.ds(r0, GLB_ONLINE_TK), :], vTl_ref[:, pl.ds(r0, GLB_ONLINE_TK)])

        carry = lax.fori_loop(0, jnp.where(is_lat, S // GLB_ONLINE_TK, 0), body, carry)
        return tuple(acc * (1.0 / l) for (m, l, acc) in carry)

    ctx_chunks = [(kx_ref, vTx_ref, r0) for r0 in range(0, Lc, GLB_TK)]
    lat_chunks = [(kl_ref, vTl_ref, r0) for r0 in range(0, S, GLB_TK)]

    def bounded_any():
        return lax.cond(is_lat, lambda: bounded(ctx_chunks + lat_chunks), lambda: bounded(ctx_chunks))

    oT = lax.cond(par_ref[1] > 0.5, bounded_any, online)
    OT = jnp.concatenate([oT[kvh][:, g * BLOCK:(g + 1) * BLOCK]
                          for kvh in range(GLB_KV_HEADS) for g in range(grp)], axis=0)
    o_ref[...] = OT.T.astype(o_ref.dtype)


def _global_call(dims, par, dqT, dk, dvT):
    B, S, Lc, R = dims
    nqb, ncb = S // BLOCK, Lc // BLOCK
    lat_blocks = B * nqb
    kvw = GLB_KV_HEADS * HEAD_DIM
    qw = GLB_HEADS * HEAD_DIM
    assert Lc % GLB_ONLINE_TK == 0 and S % GLB_ONLINE_TK == 0 and GLB_ONLINE_TK % GLB_TK == 0

    def q_blk(b, j):
        return jnp.where(j < nqb, b * nqb + j, lat_blocks + b * ncb + (j - nqb))

    return pl.pallas_call(
        functools.partial(_global_kernel, dims),
        grid=(B, nqb + ncb),
        in_specs=[pl.BlockSpec(memory_space=pltpu.SMEM),
                  pl.BlockSpec((qw, BLOCK), lambda b, j: (0, q_blk(b, j))),
                  pl.BlockSpec((S, kvw), lambda b, j: (b, 0)),
                  pl.BlockSpec((Lc, kvw), lambda b, j: (B * S // Lc + b, 0)),
                  pl.BlockSpec((kvw, S), lambda b, j: (0, b)),
                  pl.BlockSpec((kvw, Lc), lambda b, j: (0, B * S // Lc + b))],
        out_specs=pl.BlockSpec((BLOCK, qw), lambda b, j: (q_blk(b, j), 0)),
        out_shape=jax.ShapeDtypeStruct((R, qw), BF16),
        compiler_params=_cparams("arbitrary", "arbitrary"),
        name="global_attn",
    )(par, dqT, dk, dk, dvT, dvT)


RET_HPS = 2


def _ret_kernel(dims, ql_ref, qx_ref, kl_ref, kx_ref, vl_ref, vx_ref, gl_ref, gx_ref, lg_ref, gn_ref,
                ol_ref, ox_ref, fl_buf, fx_buf):
    B, S, Lc, R = dims
    C = RET_CHUNK
    hp = pl.program_id(1)
    pos_r = lax.broadcasted_iota(jnp.int32, (C, C), 0).astype(F32)
    pos_c = lax.broadcasted_iota(jnp.int32, (C, C), 1).astype(F32)
    diff = pos_r - pos_c
    col_pos = lax.broadcasted_iota(jnp.int32, (C, 1), 0).astype(F32)

    tabs = {}
    for dr in range(2):
        for hh in range(RET_HPS):
            lg = lg_ref[dr, hp * RET_HPS + hh]
            dd = diff if dr == 0 else -diff
            dmat = jnp.where(dd >= 0, jnp.exp(lg * jnp.maximum(dd, 0.0)), 0.0)
            p_eff = col_pos if dr == 0 else (C - 1.0) - col_pos
            kdec = jnp.exp(lg * ((C - 1.0) - p_eff))
            qdec = jnp.exp(lg * (p_eff + 1.0))
            cdec = jnp.exp(jnp.zeros((1, RET_V_DIM), F32) + lg * float(C))
            tabs[dr, hh] = (dmat, kdec, qdec, cdec)

    def step(q_ref, k_ref, v_ref, r0, dr, hh, state):
        dmat, kdec, qdec, cdec = tabs[dr, hh]
        qs = slice(hh * RET_QK_DIM, (hh + 1) * RET_QK_DIM)
        vs = slice(hh * RET_V_DIM, (hh + 1) * RET_V_DIM)
        qc = q_ref[pl.ds(r0, C), qs]
        kc = k_ref[pl.ds(r0, C), qs]
        vc = v_ref[pl.ds(r0, C), vs]
        sc = lax.dot_general(qc, kc, (((1,), (1,)), ((), ())), preferred_element_type=F32) * dmat
        inner = jnp.dot(sc.astype(BF16), vc, preferred_element_type=F32)
        qd = (qc.astype(F32) * qdec).astype(BF16)
        cross = jnp.dot(qd, state.astype(BF16), preferred_element_type=F32)
        kd = (kc.astype(F32) * kdec).astype(BF16)
        u = lax.dot_general(kd, vc, (((0,), (0,)), ((), ())), preferred_element_type=F32)
        return inner + cross, cdec * state + u

    def sweep(q_ref, k_ref, v_ref, f_buf, n_chunks, states):
        def body(i, st):
            st = list(st)
            rf = pl.multiple_of(i * C, C)
            rb = pl.multiple_of((n_chunks - 1 - i) * C, C)
            for hh in range(RET_HPS):
                vs = slice(hh * RET_V_DIM, (hh + 1) * RET_V_DIM)
                of, st[hh] = step(q_ref, k_ref, v_ref, rf, 0, hh, st[hh])
                f_buf[0, pl.ds(rf, C), vs] = of
                ob, st[RET_HPS + hh] = step(q_ref, k_ref, v_ref, rb, 1, hh, st[RET_HPS + hh])
                f_buf[1, pl.ds(rb, C), vs] = ob
            return tuple(st)
        return lax.fori_loop(0, n_chunks, body, states)

    def finish(f_buf, g_ref, o_ref, n_chunks):
        def body(i, carry):
            r0 = pl.multiple_of(i * C, C)
            for hh in range(RET_HPS):
                vs = slice(hh * RET_V_DIM, (hh + 1) * RET_V_DIM)
                o = f_buf[0, pl.ds(r0, C), vs] + f_buf[1, pl.ds(r0, C), vs]
                mu = jnp.mean(o, axis=-1, keepdims=True)
                xc = o - mu
                var = jnp.mean(xc * xc, axis=-1, keepdims=True)
                y = xc * lax.rsqrt(var + EPS) * gn_ref[:, vs]
                o_ref[pl.ds(r0, C), vs] = (_silu(g_ref[pl.ds(r0, C), vs]) * y).astype(o_ref.dtype)
            return carry
        lax.fori_loop(0, n_chunks, body, 0)

    zero = jnp.zeros((RET_QK_DIM, RET_V_DIM), F32)
    states = sweep(qx_ref, kx_ref, vx_ref, fx_buf, Lc // C, (zero,) * (2 * RET_HPS))
    sweep(ql_ref, kl_ref, vl_ref, fl_buf, S // C, states)
    finish(fx_buf, gx_ref, ox_ref, Lc // C)
    finish(fl_buf, gl_ref, ol_ref, S // C)


def _ret_call(dims, cq, ck, cv, cg, log_gamma, gn_g):
    B, S, Lc, R = dims
    qw = RET_HPS * RET_QK_DIM
    vw = RET_HPS * RET_V_DIM
    nhp = RET_HEADS // RET_HPS
    xoff = B * S // Lc

    def lat(w):
        return pl.BlockSpec((S, w), lambda b, h: (b, h))

    def ctx(w):
        return pl.BlockSpec((Lc, w), lambda b, h: (xoff + b, h))

    return pl.pallas_call(
        functools.partial(_ret_kernel, dims),
        grid=(B, nhp),
        in_specs=[lat(qw), ctx(qw), lat(qw), ctx(qw), lat(vw), ctx(vw), lat(vw), ctx(vw),
                  pl.BlockSpec(memory_space=pltpu.SMEM),
                  pl.BlockSpec((1, vw), lambda b, h: (0, h))],
        out_specs=[pl.BlockSpec((S, vw), lambda b, h: (b, h)),
                   pl.BlockSpec((Lc, vw), lambda b, h: (b, h))],
        out_shape=[jax.ShapeDtypeStruct((B * S, RET_HEADS * RET_V_DIM), BF16),
                   jax.ShapeDtypeStruct((B * Lc, RET_HEADS * RET_V_DIM), BF16)],
        scratch_shapes=[pltpu.VMEM((2, S, vw), F32), pltpu.VMEM((2, Lc, vw), F32)],
        compiler_params=_cparams("arbitrary", "arbitrary"),
        name="retention",
    )(cq, cq, ck, ck, cv, cv, cg, cg, log_gamma, gn_g.reshape(1, -1))


def _merge_kernel(ya_ref, yb_ref, yr_ref, yd_ref, gt_ref, x_ref, mod_ref, g_ref, wbr_ref, wo_ref, o_ref):
    acc = None
    for i, br in enumerate((ya_ref, yb_ref, yr_ref, yd_ref)):
        t = gt_ref[:, i * D_MODEL:(i + 1) * D_MODEL].astype(F32) * jnp.dot(
            br[...], wbr_ref[i], preferred_element_type=F32)
        acc = t if acc is None else acc + t
    y = jnp.dot(acc.astype(BF16), wo_ref[...], preferred_element_type=F32)
    r = y * lax.rsqrt(jnp.mean(y * y, axis=-1, keepdims=True) + EPS) * g_ref[...]
    gate = mod_ref[0][:, 2 * D_MODEL:3 * D_MODEL]
    o_ref[...] = x_ref[...] + gate * r


def _merge_call(dims, ya, yb, yr, yd, gates, xall, mod, g, w_br, w_o, tm):
    B, S, Lc, R = dims

    def row(i):
        return (i, 0)

    br = pl.BlockSpec((tm, BRANCH_DIM), row)
    return pl.pallas_call(
        _merge_kernel,
        grid=(R // tm,),
        in_specs=[br, br, br, br,
                  pl.BlockSpec((tm, N_BRANCH * D_MODEL), row),
                  pl.BlockSpec((tm, D_MODEL), row),
                  pl.BlockSpec((1, 1, 6 * D_MODEL), lambda i: (jnp.minimum(i * tm // S, B), 0, 0)),
                  pl.BlockSpec((1, D_MODEL), lambda i: (0, 0)),
                  _resident((N_BRANCH, BRANCH_DIM, D_MODEL)),
                  _resident((D_MODEL, D_MODEL))],
        out_specs=pl.BlockSpec((tm, D_MODEL), row),
        out_shape=jax.ShapeDtypeStruct((R, D_MODEL), F32),
        compiler_params=_cparams("arbitrary"),
        name="merge",
    )(ya, yb, yr, yd, gates, xall, mod, g, w_br, w_o)


FFN_HALO = 8
FFN_CN = 256


def _ffn_kernel(dims, tm, x_ref, xp_ref, xn_ref, mod_ref, g2_ref, g3_ref, wup_ref, dw_ref, db_ref, wdn_ref,
                o_ref, ua_buf, ub_buf, acc_buf):
    B, S, Lc, R = dims
    first, last = _seq_edges(pl.program_id(0), tm, B * S, S, Lc)
    m = mod_ref[0]
    g2 = g2_ref[...]
    x = x_ref[...]
    hp = jnp.where(first, 0.0, _modulated(xp_ref[...], g2, m, 1))
    hn = jnp.where(last, 0.0, _modulated(xn_ref[...], g2, m, 1))
    h = jnp.concatenate([hp, _modulated(x, g2, m, 1), hn], axis=0).astype(BF16)

    def tap(buf, col0, k):
        return dw_ref[k:k + 1, col0:col0 + FFN_CN] * buf[FFN_HALO - 1 + k:FFN_HALO - 1 + k + tm, :]

    for j in range(D_FF // FFN_CN):
        ca, cb = j * FFN_CN, D_FF + j * FFN_CN
        ua_buf[...] = jnp.dot(h, wup_ref[:, ca:ca + FFN_CN], preferred_element_type=F32)
        ub_buf[...] = jnp.dot(h, wup_ref[:, cb:cb + FFN_CN], preferred_element_type=F32)
        a = tap(ua_buf, ca, 0) + tap(ua_buf, ca, 1) + tap(ua_buf, ca, 2) + db_ref[:, ca:ca + FFN_CN]
        b = tap(ub_buf, cb, 0) + tap(ub_buf, cb, 1) + tap(ub_buf, cb, 2) + db_ref[:, cb:cb + FFN_CN]
        t = (_silu(a) * b).astype(BF16)
        part = jnp.dot(t, wdn_ref[ca:ca + FFN_CN, :], preferred_element_type=F32)
        if j == 0:
            acc_buf[...] = part
        else:
            acc_buf[...] += part

    y = acc_buf[...]
    r = y * lax.rsqrt(jnp.mean(y * y, axis=-1, keepdims=True) + EPS) * g3_ref[...]
    o_ref[...] = x + m[:, 5 * D_MODEL:6 * D_MODEL] * r


def _ffn_call(dims, xall, mod, g2, g3, f_up, f_dw, f_dw_b, f_down, tm):
    B, S, Lc, R = dims
    hb = tm // FFN_HALO
    nhb = R // FFN_HALO
    return pl.pallas_call(
        functools.partial(_ffn_kernel, dims, tm),
        grid=(R // tm,),
        in_specs=[
            pl.BlockSpec((tm, D_MODEL), lambda i: (i, 0)),
            pl.BlockSpec((FFN_HALO, D_MODEL), lambda i: (jnp.maximum(i * hb - 1, 0), 0)),
            pl.BlockSpec((FFN_HALO, D_MODEL), lambda i: (jnp.minimum((i + 1) * hb, nhb - 1), 0)),
            pl.BlockSpec((1, 1, 6 * D_MODEL), lambda i: (jnp.minimum(i * tm // S, B), 0, 0)),
            pl.BlockSpec((1, D_MODEL), lambda i: (0, 0)),
            pl.BlockSpec((1, D_MODEL), lambda i: (0, 0)),
            _resident((D_MODEL, 2 * D_FF)),
            pl.BlockSpec((FFN_CONV_WIDTH, 2 * D_FF), lambda i: (0, 0)),
            pl.BlockSpec((1, 2 * D_FF), lambda i: (0, 0)),
            _resident((D_FF, D_MODEL)),
        ],
        out_specs=pl.BlockSpec((tm, D_MODEL), lambda i: (i, 0)),
        out_shape=jax.ShapeDtypeStruct((R, D_MODEL), F32),
        scratch_shapes=[pltpu.VMEM((tm + 2 * FFN_HALO, FFN_CN), F32),
                        pltpu.VMEM((tm + 2 * FFN_HALO, FFN_CN), F32),
                        pltpu.VMEM((tm, D_MODEL), F32)],
        compiler_params=_cparams("arbitrary"),
        name="conv_ffn",
    )(xall, xall, xall, mod, g2, g3, f_up, f_dw, f_dw_b.reshape(1, -1), f_down)


def _rope_tables(S, pad_rows):
    rows = S // GRID_W
    row = jnp.repeat(jnp.arange(rows), GRID_W).astype(F32)
    col = jnp.tile(jnp.arange(GRID_W), rows).astype(F32)
    half = HEAD_DIM // 2
    inv = ROPE_BASE ** (-jnp.arange(0, half, 2, dtype=F32) / half)
    ang = jnp.concatenate([row[:, None] * inv, col[:, None] * inv], axis=-1)
    cos, sin = jnp.cos(ang), jnp.sin(ang)
    zero = jnp.zeros_like(sin)
    cos_h = jnp.concatenate([cos, cos], axis=-1)
    sa_h = jnp.concatenate([-sin, zero], axis=-1)
    sb_h = jnp.concatenate([zero, sin], axis=-1)
    reps = LANES // HEAD_DIM

    def slab(t, fill):
        t = jnp.tile(t, (1, reps))
        return jnp.concatenate([t, jnp.full((pad_rows, LANES), fill, F32)], axis=0)

    return slab(cos_h, 1.0), slab(sa_h, 0.0), slab(sb_h, 0.0)


def kernel(x, c, ctx, c_ctx, ada_w, ada_b, norm_g, w_in, a_dw, a_dw_b, a_ln_g, a_ln_b, b_sink,
           c_decay_logit, c_gn_g, d_qn_g, d_kn_g, w_br, w_o, f_up, f_dw, f_dw_b, f_down):
    B, S, _ = x.shape
    Lc = ctx.shape[1]
    R = B * (S + Lc)
    dims = (B, S, Lc, R)
    tm = 256
    assert B + 1 <= MOD_ROWS and S % tm == 0 and (B * Lc) % tm == 0 and Lc % BLOCK == 0 and S % Lc == 0

    cvec = jnp.concatenate([c, c_ctx[None, :], jnp.zeros((MOD_ROWS - B - 1, D_MODEL), F32)], axis=0)
    mods = _ada_call(cvec, ada_w, ada_b).reshape(DEPTH, MOD_ROWS, 1, 6 * D_MODEL)
    cos_t, sa_t, sb_t = _rope_tables(S, tm)
    reps = LANES // HEAD_DIM
    log_gamma = jnp.log(jax.nn.sigmoid(c_decay_logit.astype(F32)))

    xall = jnp.concatenate([x.reshape(B * S, D_MODEL), ctx.reshape(B * Lc, D_MODEL)], axis=0)
    for l in range(DEPTH):
        mod = mods[l]
        ng = norm_g[l].reshape(4, 1, D_MODEL)
        (z, bq, bk, bv, cq, ck, cv, cg, dq, dk, dv, gates) = _proj_call(
            dims, xall, mod, ng[0], w_in[l].astype(BF16), cos_t, sa_t, sb_t,
            jnp.tile(d_qn_g[l], reps)[None, :], jnp.tile(d_kn_g[l], reps)[None, :], tm)
        ya = _conv_call(dims, z, a_dw[l], a_dw_b[l], a_ln_g[l], a_ln_b[l], 256)
        yb = _window_call(dims, bq, bk, bv, b_sink[l])
        yr_l, yr_x = _ret_call(dims, cq, ck, cv, cg, log_gamma[l], c_gn_g[l])
        yr = jnp.concatenate([yr_l, yr_x], axis=0)
        bound = 1.02 * HEAD_DIM * (HEAD_DIM ** -0.5) * jnp.max(jnp.abs(d_qn_g[l])) * jnp.max(jnp.abs(d_kn_g[l]))
        par = jnp.stack([bound * LOG2E, (bound <= GLB_MAX_SAFE_BOUND).astype(F32)])
        yd = _global_call(dims, par, dq, dk, dv)
        xall = _merge_call(dims, ya, yb, yr, yd, gates, xall, mod, ng[1],
                           w_br[l].astype(BF16), w_o[l].astype(BF16), tm)
        xall = _ffn_call(dims, xall, mod, ng[2], ng[3], f_up[l].astype(BF16), f_dw[l], f_dw_b[l],
                         f_down[l].astype(BF16), tm)
    return xall[:B * S].reshape(B, S, D_MODEL)
```

```python
import functools

import jax
import jax.numpy as jnp
import numpy as np
from jax import lax
from jax.experimental import pallas as pl
from jax.experimental.pallas import tpu as pltpu

D_MODEL = 1024
DEPTH = 4
GRID_W = 64
HEAD_DIM = 64
BLOCK = 128
ROPE_BASE = 10000.0
EPS = 1e-6
NEG_INF = -1e30
CONV_DIM = 512
CONV_WIDTH = 31
WIN_HEADS = 8
WIN_KV_HEADS = 2
WINDOW = 128
RET_HEADS = 4
RET_QK_DIM = 64
RET_V_DIM = 128
RET_CHUNK = 128
GLB_HEADS = 8
GLB_KV_HEADS = 2
N_BRANCH = 4
BRANCH_DIM = 512
D_FF = 2816
FFN_CONV_WIDTH = 3

IN_SIZES = (2 * CONV_DIM,
            WIN_HEADS * HEAD_DIM, WIN_KV_HEADS * HEAD_DIM, WIN_KV_HEADS * HEAD_DIM,
            RET_HEADS * RET_QK_DIM, RET_HEADS * RET_QK_DIM, RET_HEADS * RET_V_DIM, RET_HEADS * RET_V_DIM,
            GLB_HEADS * HEAD_DIM, GLB_KV_HEADS * HEAD_DIM, GLB_KV_HEADS * HEAD_DIM,
            N_BRANCH * D_MODEL)
IN_DIM = sum(IN_SIZES)
IN_OFF = tuple(int(v) for v in np.cumsum((0,) + IN_SIZES)[:-1])
(OFF_AU, OFF_BQ, OFF_BK, OFF_BV, OFF_CQ, OFF_CK, OFF_CV, OFF_CG,
 OFF_DQ, OFF_DK, OFF_DV, OFF_GATES) = IN_OFF

LANES = 128
SUBLANES = 8
MOD_ROWS = 16
VMEM_LIMIT = 56 * 1024 * 1024
F32 = jnp.float32
BF16 = jnp.bfloat16
LOG2E = 1.4426950408889634


def _sigmoid(v):
    return 1.0 / (1.0 + jnp.exp(-v))


def _silu(v):
    return v * _sigmoid(v)


def _cparams(*sem):
    return pltpu.CompilerParams(dimension_semantics=sem, vmem_limit_bytes=VMEM_LIMIT)


def _resident(shape):
    nd = len(shape)
    return pl.BlockSpec(shape, lambda *_: (0,) * nd, pipeline_mode=pl.Buffered(1))


def _ada_kernel(c_ref, w_ref, b_ref, o_ref):
    a = _silu(c_ref[...]).astype(BF16)
    w = w_ref[0].astype(BF16)
    o_ref[0] = jnp.dot(a, w, preferred_element_type=F32) + b_ref[0]


def _ada_call(cvec, ada_w, ada_b):
    tn = 1536
    n6 = 6 * D_MODEL
    return pl.pallas_call(
        _ada_kernel,
        grid=(DEPTH, n6 // tn),
        in_specs=[
            pl.BlockSpec((MOD_ROWS, D_MODEL), lambda l, j: (0, 0)),
            pl.BlockSpec((1, D_MODEL, tn), lambda l, j: (l, 0, j)),
            pl.BlockSpec((1, 1, tn), lambda l, j: (l, 0, j)),
        ],
        out_specs=pl.BlockSpec((1, MOD_ROWS, tn), lambda l, j: (l, 0, j)),
        out_shape=jax.ShapeDtypeStruct((DEPTH, MOD_ROWS, n6), F32),
        compiler_params=_cparams("arbitrary", "arbitrary"),
        name="ada_mod",
    )(cvec, ada_w, ada_b.reshape(DEPTH, 1, n6))


def _modulated(x, g, m, which):
    o = 3 * which * D_MODEL
    y = x * lax.rsqrt(jnp.mean(x * x, axis=-1, keepdims=True) + EPS) * g
    return y * (1.0 + m[:, o + D_MODEL:o + 2 * D_MODEL]) + m[:, o:o + D_MODEL]


def _rope128(v, cos, sa, sb):
    return v * cos + pltpu.roll(v, LANES - HEAD_DIM // 2, 1) * sa + pltpu.roll(v, HEAD_DIM // 2, 1) * sb


def _head_rms128(v, gain):
    lane = lax.broadcasted_iota(jnp.int32, (1, LANES), 1)
    lo = lane < HEAD_DIM
    sq = v * v
    s0 = jnp.sum(jnp.where(lo, sq, 0.0), axis=-1, keepdims=True)
    s1 = jnp.sum(jnp.where(lo, 0.0, sq), axis=-1, keepdims=True)
    ms = jnp.where(lo, s0, s1) * (1.0 / HEAD_DIM)
    return v * lax.rsqrt(ms + EPS) * gain


def _proj_kernel(x_ref, mod_ref, g_ref, w_ref, cos_ref, sa_ref, sb_ref, qn_ref, kn_ref,
                 z_ref, bq_ref, bk_ref, bv_ref, cq_ref, ck_ref, cv_ref, cg_ref,
                 dq_ref, dk_ref, dv_ref, gt_ref):
    h = _modulated(x_ref[...], g_ref[...], mod_ref[0], 0).astype(BF16)

    def proj(off, width):
        return jnp.dot(h, w_ref[:, off:off + width], preferred_element_type=F32)

    u = proj(OFF_AU, 2 * CONV_DIM)
    z_ref[...] = u[:, :CONV_DIM] * _sigmoid(u[:, CONV_DIM:])

    cos, sa, sb = cos_ref[...], sa_ref[...], sb_ref[...]
    scale = HEAD_DIM ** -0.5

    def roped(off, width, out_ref, gain_ref, mul, feature_major=False):
        v = proj(off, width)
        for c in range(width // LANES):
            s = v[:, c * LANES:(c + 1) * LANES]
            if gain_ref is not None:
                s = _head_rms128(s, gain_ref[...])
            s = _rope128(s, cos, sa, sb) * mul
            if feature_major:
                out_ref[c * LANES:(c + 1) * LANES, :] = s.T.astype(out_ref.dtype)
            else:
                out_ref[:, c * LANES:(c + 1) * LANES] = s.astype(out_ref.dtype)

    roped(OFF_BQ, WIN_HEADS * HEAD_DIM, bq_ref, None, scale * LOG2E, feature_major=True)
    roped(OFF_BK, WIN_KV_HEADS * HEAD_DIM, bk_ref, None, 1.0)
    bv_ref[...] = proj(OFF_BV, WIN_KV_HEADS * HEAD_DIM).astype(BF16).T

    cq_ref[...] = proj(OFF_CQ, RET_HEADS * RET_QK_DIM).astype(BF16)
    ck_ref[...] = (proj(OFF_CK, RET_HEADS * RET_QK_DIM) * (RET_QK_DIM ** -0.5)).astype(BF16)
    cv_ref[...] = proj(OFF_CV, RET_HEADS * RET_V_DIM).astype(BF16)
    cg_ref[...] = proj(OFF_CG, RET_HEADS * RET_V_DIM)

    roped(OFF_DQ, GLB_HEADS * HEAD_DIM, dq_ref, qn_ref, scale * LOG2E, feature_major=True)
    roped(OFF_DK, GLB_KV_HEADS * HEAD_DIM, dk_ref, kn_ref, 1.0)
    dv_ref[...] = proj(OFF_DV, GLB_KV_HEADS * HEAD_DIM).astype(BF16).T

    for i in range(N_BRANCH):
        gl = proj(OFF_GATES + i * D_MODEL, D_MODEL)
        gt_ref[:, i * D_MODEL:(i + 1) * D_MODEL] = _sigmoid(gl).astype(BF16)


def _proj_call(dims, xall, mod, g, w_in, cos_t, sa_t, sb_t, qn2, kn2, tm):
    B, S, Lc, R = dims
    nlat = B * S // tm
    n_rope_lat = S // tm

    def row(i):
        return (i, 0)

    def mod_idx(i):
        return (jnp.minimum(i * tm // S, B), 0, 0)

    def rope_idx(i):
        return (jnp.where(i < nlat, i % n_rope_lat, n_rope_lat), 0)

    outs = [(CONV_DIM, F32, False), (512, BF16, True), (128, BF16, False), (128, BF16, True),
            (256, BF16, False), (256, BF16, False), (512, BF16, False), (512, F32, False),
            (512, BF16, True), (128, BF16, False), (128, BF16, True), (4 * D_MODEL, BF16, False)]
    return pl.pallas_call(
        _proj_kernel,
        grid=(R // tm,),
        in_specs=[
            pl.BlockSpec((tm, D_MODEL), row),
            pl.BlockSpec((1, 1, 6 * D_MODEL), mod_idx),
            pl.BlockSpec((1, D_MODEL), lambda i: (0, 0)),
            _resident((D_MODEL, IN_DIM)),
            pl.BlockSpec((tm, LANES), rope_idx),
            pl.BlockSpec((tm, LANES), rope_idx),
            pl.BlockSpec((tm, LANES), rope_idx),
            pl.BlockSpec((1, LANES), lambda i: (0, 0)),
            pl.BlockSpec((1, LANES), lambda i: (0, 0)),
        ],
        out_specs=[pl.BlockSpec((w, tm), lambda i: (0, i)) if fm else pl.BlockSpec((tm, w), row)
                   for w, _, fm in outs],
        out_shape=[jax.ShapeDtypeStruct((w, R) if fm else (R, w), dt) for w, dt, fm in outs],
        compiler_params=_cparams("arbitrary"),
        name="proj",
    )(xall, mod, g, w_in, cos_t, sa_t, sb_t, qn2, kn2)


CONV_HALO = 16
CONV_ROWS = 32


def _seq_edges(i, tile, n_lat_rows, S, Lc):
    r0 = i * tile
    in_lat = r0 < n_lat_rows
    pos = jnp.where(in_lat, r0 % S, (r0 - n_lat_rows) % Lc)
    length = jnp.where(in_lat, S, Lc)
    return pos == 0, pos + tile == length


def _conv_kernel(dims, tc, z_ref, zp_ref, zn_ref, w_ref, b_ref, lg_ref, lb_ref, o_ref, buf):
    B, S, Lc, R = dims
    first, last = _seq_edges(pl.program_id(0), tc, B * S, S, Lc)
    rows = tc + 2 * CONV_HALO
    zpad = jnp.concatenate([jnp.where(first, 0.0, zp_ref[...]), z_ref[...],
                            jnp.where(last, 0.0, zn_ref[...])], axis=0)
    buf[0] = zpad
    for r in range(1, SUBLANES):
        buf[r] = pltpu.roll(zpad, rows - r, 0)
    pad = (CONV_WIDTH - 1) // 2
    bias, lg, lb = b_ref[...], lg_ref[...], lb_ref[...]

    for c in range(tc // CONV_ROWS):
        r0 = c * CONV_ROWS
        acc = jnp.zeros((CONV_ROWS, CONV_DIM), F32) + bias
        for k in range(CONV_WIDTH):
            off = CONV_HALO - pad + k
            base = r0 + (off // SUBLANES) * SUBLANES
            acc = acc + w_ref[k:k + 1, :] * buf[off % SUBLANES, base:base + CONV_ROWS, :]
        mu = jnp.mean(acc, axis=-1, keepdims=True)
        xc = acc - mu
        var = jnp.mean(xc * xc, axis=-1, keepdims=True)
        y = xc * lax.rsqrt(var + EPS) * lg + lb
        o_ref[r0:r0 + CONV_ROWS, :] = _silu(y).astype(o_ref.dtype)


def _conv_call(dims, z, a_dw, a_dw_b, a_ln_g, a_ln_b, tc):
    B, S, Lc, R = dims
    hb = tc // CONV_HALO
    nhb = R // CONV_HALO
    return pl.pallas_call(
        functools.partial(_conv_kernel, dims, tc),
        grid=(R // tc,),
        in_specs=[
            pl.BlockSpec((tc, CONV_DIM), lambda i: (i, 0)),
            pl.BlockSpec((CONV_HALO, CONV_DIM), lambda i: (jnp.maximum(i * hb - 1, 0), 0)),
            pl.BlockSpec((CONV_HALO, CONV_DIM), lambda i: (jnp.minimum((i + 1) * hb, nhb - 1), 0)),
            pl.BlockSpec((CONV_WIDTH, CONV_DIM), lambda i: (0, 0)),
            pl.BlockSpec((1, CONV_DIM), lambda i: (0, 0)),
            pl.BlockSpec((1, CONV_DIM), lambda i: (0, 0)),
            pl.BlockSpec((1, CONV_DIM), lambda i: (0, 0)),
        ],
        out_specs=pl.BlockSpec((tc, CONV_DIM), lambda i: (i, 0)),
        out_shape=jax.ShapeDtypeStruct((R, CONV_DIM), BF16),
        scratch_shapes=[pltpu.VMEM((SUBLANES, tc + 2 * CONV_HALO, CONV_DIM), F32)],
        compiler_params=_cparams("arbitrary"),
        name="conv_branch",
    )(z, z, z, a_dw, a_dw_b.reshape(1, -1), a_ln_g.reshape(1, -1), a_ln_b.reshape(1, -1))


def _stacked_q_weights(qT, kv_heads, grp):
    zeros = jnp.zeros((HEAD_DIM, grp * BLOCK), BF16)
    out = []
    for kvh in range(kv_heads):
        piece = jnp.concatenate([qT[(kvh * grp + g) * HEAD_DIM:(kvh * grp + g + 1) * HEAD_DIM, :]
                                 for g in range(grp)], axis=1)
        out.append(jnp.concatenate([piece, zeros] if kvh == 0 else [zeros, piece], axis=0))
    return out


def _unstack_heads(oT, kv_heads, grp):
    OT = jnp.concatenate([oT[kvh][:, g * BLOCK:(g + 1) * BLOCK]
                          for kvh in range(kv_heads) for g in range(grp)], axis=0)
    return OT.T


def _window_kernel(dims, qT_ref, kp_ref, kc_ref, kn_ref, kx_ref, vp_ref, vc_ref, vn_ref, vx_ref,
                   sink_ref, o_ref):
    B, S, Lc, R = dims
    nqb = S // BLOCK
    j = pl.program_id(1)
    is_lat = j < nqb
    grp = WIN_HEADS // WIN_KV_HEADS
    nk = 3 * BLOCK + Lc

    key = lax.broadcasted_iota(jnp.int32, (nk, BLOCK), 0)
    qry = lax.broadcasted_iota(jnp.int32, (nk, BLOCK), 1)
    d = key - qry
    lo = jnp.where(j >= 1, 0, BLOCK)
    hi = jnp.where(is_lat, jnp.where(j <= nqb - 2, 3 * BLOCK, 2 * BLOCK), 0)
    band = (d >= BLOCK - WINDOW) & (d <= BLOCK + WINDOW) & (key >= lo) & (key < hi)
    bias = jnp.where(band | (key >= 3 * BLOCK), 0.0, NEG_INF)
    bias = jnp.concatenate([bias] * grp, axis=1)

    W = _stacked_q_weights(qT_ref[...], WIN_KV_HEADS, grp)
    kk = jnp.concatenate([kp_ref[...], kc_ref[...], kn_ref[...], kx_ref[...]], axis=0)
    vT = jnp.concatenate([vp_ref[...], vc_ref[...], vn_ref[...], vx_ref[...]], axis=1)
    oT = []
    scores = [jnp.dot(kk, W[kvh], preferred_element_type=F32) for kvh in range(WIN_KV_HEADS)]
    for kvh in range(WIN_KV_HEADS):
        sk = jnp.concatenate([jnp.full((1, BLOCK), sink_ref[kvh * grp + g] * LOG2E, F32)
                              for g in range(grp)], axis=1)
        sT = scores[kvh] + bias
        m = jnp.maximum(jnp.max(sT, axis=0, keepdims=True), sk)
        pT = jnp.exp2(sT - m)
        l = jnp.sum(pT, axis=0, keepdims=True) + jnp.exp2(sk - m)
        oT.append(jnp.dot(vT[kvh * HEAD_DIM:(kvh + 1) * HEAD_DIM, :], pT.astype(BF16),
                          preferred_element_type=F32) * (1.0 / l))
    o_ref[...] = _unstack_heads(oT, WIN_KV_HEADS, grp).astype(o_ref.dtype)


def _window_call(dims, bqT, bk, bvT, sink):
    B, S, Lc, R = dims
    nqb, ncb = S // BLOCK, Lc // BLOCK
    lat_blocks = B * nqb
    kvw = WIN_KV_HEADS * HEAD_DIM
    qw = WIN_HEADS * HEAD_DIM

    def q_blk(b, j):
        return jnp.where(j < nqb, b * nqb + j, lat_blocks + b * ncb + (j - nqb))

    def k_blk(b, j, delta):
        return b * nqb + jnp.clip(j + delta, 0, nqb - 1)

    kspec = [pl.BlockSpec((BLOCK, kvw), lambda b, j, d=d: (k_blk(b, j, d), 0)) for d in (-1, 0, 1)]
    kspec.append(pl.BlockSpec((Lc, kvw), lambda b, j: (B * S // Lc + b, 0)))
    vspec = [pl.BlockSpec((kvw, BLOCK), lambda b, j, d=d: (0, k_blk(b, j, d))) for d in (-1, 0, 1)]
    vspec.append(pl.BlockSpec((kvw, Lc), lambda b, j: (0, B * S // Lc + b)))
    return pl.pallas_call(
        functools.partial(_window_kernel, dims),
        grid=(B, nqb + ncb),
        in_specs=[pl.BlockSpec((qw, BLOCK), lambda b, j: (0, q_blk(b, j)))] + kspec + vspec
        + [pl.BlockSpec(memory_space=pltpu.SMEM)],
        out_specs=pl.BlockSpec((BLOCK, qw), lambda b, j: (q_blk(b, j), 0)),
        out_shape=jax.ShapeDtypeStruct((R, qw), BF16),
        compiler_params=_cparams("arbitrary", "arbitrary"),
        name="window_attn",
    )(bqT, bk, bk, bk, bk, bvT, bvT, bvT, bvT, sink)


GLB_TK = 256
GLB_ONLINE_TK = 256
GLB_MAX_SAFE_BOUND = 40.0


def _global_kernel(dims, par_ref, qT_ref, kl_ref, kx_ref, vTl_ref, vTx_ref, o_ref):
    B, S, Lc, R = dims
    nqb = S // BLOCK
    is_lat = pl.program_id(1) < nqb
    grp = GLB_HEADS // GLB_KV_HEADS
    cols = grp * BLOCK
    shift = par_ref[0]
    W = _stacked_q_weights(qT_ref[...], GLB_KV_HEADS, grp)

    def scores(kk):
        return tuple(jnp.dot(kk, W[kvh], preferred_element_type=F32) for kvh in range(GLB_KV_HEADS))

    def v_rows(vT, kvh):
        return vT[kvh * HEAD_DIM:(kvh + 1) * HEAD_DIM, :]

    def bounded(chunks):
        ls = [jnp.zeros((1, cols), F32)] * GLB_KV_HEADS
        accs = [jnp.zeros((HEAD_DIM, cols), F32)] * GLB_KV_HEADS
        s_cur = scores(chunks[0][0][chunks[0][2]:chunks[0][2] + GLB_TK, :])
        for i, (k_ref, v_ref, r0) in enumerate(chunks):
            s_next = None
            if i + 1 < len(chunks):
                nk, _, nr = chunks[i + 1]
                s_next = scores(nk[nr:nr + GLB_TK, :])
            vT = v_ref[:, r0:r0 + GLB_TK]
            for kvh in range(GLB_KV_HEADS):
                pT = jnp.exp2(s_cur[kvh] - shift)
                ls[kvh] = ls[kvh] + jnp.sum(pT, axis=0, keepdims=True)
                accs[kvh] = accs[kvh] + jnp.dot(v_rows(vT, kvh), pT.astype(BF16), preferred_element_type=F32)
            s_cur = s_next
        return tuple(acc * (1.0 / l) for l, acc in zip(ls, accs))

    def online():
        def update(carry, kk, vT):
            new = []
            for kvh, sT in enumerate(scores(kk)):
                m, l, acc = carry[kvh]
                m_new = jnp.maximum(m, jnp.max(sT, axis=0, keepdims=True))
                alpha = jnp.exp2(m - m_new)
                pT = jnp.exp2(sT - m_new)
                l = alpha * l + jnp.sum(pT, axis=0, keepdims=True)
                acc = alpha * acc + jnp.dot(v_rows(vT, kvh), pT.astype(BF16), preferred_element_type=F32)
                new.append((m_new, l, acc))
            return tuple(new)

        init = tuple((jnp.full((1, cols), NEG_INF, F32), jnp.zeros((1, cols), F32),
                      jnp.zeros((HEAD_DIM, cols), F32)) for _ in range(GLB_KV_HEADS))
        carry = init
        for r0 in range(0, Lc, GLB_ONLINE_TK):
            carry = update(carry, kx_ref[r0:r0 + GLB_ONLINE_TK, :], vTx_ref[:, r0:r0 + GLB_ONLINE_TK])

        def body(c, carry):
            r0 = pl.multiple_of(c * GLB_ONLINE_TK, GLB_ONLINE_TK)
            return update(carry, kl_ref[pl.ds(r0, GLB_ONLINE_TK), :], vTl_ref[:, pl.ds(r0, GLB_ONLINE_TK)])

        carry = lax.fori_loop(0, jnp.where(is_lat, S // GLB_ONLINE_TK, 0), body, carry)
        return tuple(acc * (1.0 / l) for (m, l, acc) in carry)

    ctx_chunks = [(kx_ref, vTx_ref, r0) for r0 in range(0, Lc, GLB_TK)]
    lat_chunks = [(kl_ref, vTl_ref, r0) for r0 in range(0, S, GLB_TK)]

    def bounded_any():
        return lax.cond(is_lat, lambda: bounded(ctx_chunks + lat_chunks), lambda: bounded(ctx_chunks))

    oT = lax.cond(par_ref[1] > 0.5, bounded_any, online)
    o_ref[...] = _unstack_heads(oT, GLB_KV_HEADS, grp).astype(o_ref.dtype)


def _global_call(dims, par, dqT, dk, dvT):
    B, S, Lc, R = dims
    nqb, ncb = S // BLOCK, Lc // BLOCK
    lat_blocks = B * nqb
    kvw = GLB_KV_HEADS * HEAD_DIM
    qw = GLB_HEADS * HEAD_DIM
    assert Lc % GLB_ONLINE_TK == 0 and S % GLB_ONLINE_TK == 0 and GLB_ONLINE_TK % GLB_TK == 0

    def q_blk(b, j):
        return jnp.where(j < nqb, b * nqb + j, lat_blocks + b * ncb + (j - nqb))

    return pl.pallas_call(
        functools.partial(_global_kernel, dims),
        grid=(B, nqb + ncb),
        in_specs=[pl.BlockSpec(memory_space=pltpu.SMEM),
                  pl.BlockSpec((qw, BLOCK), lambda b, j: (0, q_blk(b, j))),
                  pl.BlockSpec((S, kvw), lambda b, j: (b, 0)),
                  pl.BlockSpec((Lc, kvw), lambda b, j: (B * S // Lc + b, 0)),
                  pl.BlockSpec((kvw, S), lambda b, j: (0, b)),
                  pl.BlockSpec((kvw, Lc), lambda b, j: (0, B * S // Lc + b))],
        out_specs=pl.BlockSpec((BLOCK, qw), lambda b, j: (q_blk(b, j), 0)),
        out_shape=jax.ShapeDtypeStruct((R, qw), BF16),
        compiler_params=_cparams("arbitrary", "arbitrary"),
        name="global_attn",
    )(par, dqT, dk, dk, dvT, dvT)


RET_HPS = 2
RET_KERNEL_CHUNK = 256


def _ret_kernel(dims, ql_ref, qx_ref, kl_ref, kx_ref, vl_ref, vx_ref, gl_ref, gx_ref, lg_ref, gn_ref,
                ol_ref, ox_ref, fl_buf, fx_buf):
    B, S, Lc, R = dims
    C = RET_KERNEL_CHUNK
    assert S % C == 0 and Lc % C == 0
    hp = pl.program_id(1)
    pos_r = lax.broadcasted_iota(jnp.int32, (C, C), 0).astype(F32)
    pos_c = lax.broadcasted_iota(jnp.int32, (C, C), 1).astype(F32)
    diff = pos_r - pos_c
    col_pos = lax.broadcasted_iota(jnp.int32, (C, 1), 0).astype(F32)

    tabs = {}
    for dr in range(2):
        for hh in range(RET_HPS):
            lg = lg_ref[dr, hp * RET_HPS + hh]
            dd = diff if dr == 0 else -diff
            dmat = jnp.where(dd >= 0, jnp.exp(lg * jnp.maximum(dd, 0.0)), 0.0)
            p_eff = col_pos if dr == 0 else (C - 1.0) - col_pos
            kdec = jnp.exp(lg * ((C - 1.0) - p_eff))
            qdec = jnp.exp(lg * (p_eff + 1.0))
            cdec = jnp.exp(jnp.zeros((1, RET_V_DIM), F32) + lg * float(C))
            tabs[dr, hh] = (dmat, kdec, qdec, cdec)

    def step(q_ref, k_ref, v_ref, r0, dr, hh, state):
        dmat, kdec, qdec, cdec = tabs[dr, hh]
        qs = slice(hh * RET_QK_DIM, (hh + 1) * RET_QK_DIM)
        vs = slice(hh * RET_V_DIM, (hh + 1) * RET_V_DIM)
        qc = q_ref[pl.ds(r0, C), qs]
        kc = k_ref[pl.ds(r0, C), qs]
        vc = v_ref[pl.ds(r0, C), vs]
        sc = lax.dot_general(qc, kc, (((1,), (1,)), ((), ())), preferred_element_type=F32) * dmat
        inner = jnp.dot(sc.astype(BF16), vc, preferred_element_type=F32)
        qd = (qc.astype(F32) * qdec).astype(BF16)
        cross = jnp.dot(qd, state.astype(BF16), preferred_element_type=F32)
        kd = (kc.astype(F32) * kdec).astype(BF16)
        u = lax.dot_general(kd, vc, (((0,), (0,)), ((), ())), preferred_element_type=F32)
        return inner + cross, cdec * state + u

    def sweep(q_ref, k_ref, v_ref, f_buf, n_chunks, states):
        def body(i, st):
            st = list(st)
            rf = pl.multiple_of(i * C, C)
            rb = pl.multiple_of((n_chunks - 1 - i) * C, C)
            for hh in range(RET_HPS):
                vs = slice(hh * RET_V_DIM, (hh + 1) * RET_V_DIM)
                of, st[hh] = step(q_ref, k_ref, v_ref, rf, 0, hh, st[hh])
                f_buf[0, pl.ds(rf, C), vs] = of
                ob, st[RET_HPS + hh] = step(q_ref, k_ref, v_ref, rb, 1, hh, st[RET_HPS + hh])
                f_buf[1, pl.ds(rb, C), vs] = ob
            return tuple(st)
        return lax.fori_loop(0, n_chunks, body, states)

    def finish(f_buf, g_ref, o_ref, n_chunks):
        def body(i, carry):
            r0 = pl.multiple_of(i * C, C)
            for hh in range(RET_HPS):
                vs = slice(hh * RET_V_DIM, (hh + 1) * RET_V_DIM)
                o = f_buf[0, pl.ds(r0, C), vs] + f_buf[1, pl.ds(r0, C), vs]
                mu = jnp.mean(o, axis=-1, keepdims=True)
                xc = o - mu
                var = jnp.mean(xc * xc, axis=-1, keepdims=True)
                y = xc * lax.rsqrt(var + EPS) * gn_ref[:, vs]
                o_ref[pl.ds(r0, C), vs] = (_silu(g_ref[pl.ds(r0, C), vs]) * y).astype(o_ref.dtype)
            return carry
        lax.fori_loop(0, n_chunks, body, 0)

    zero = jnp.zeros((RET_QK_DIM, RET_V_DIM), F32)
    states = sweep(qx_ref, kx_ref, vx_ref, fx_buf, Lc // C, (zero,) * (2 * RET_HPS))
    sweep(ql_ref, kl_ref, vl_ref, fl_buf, S // C, states)
    finish(fx_buf, gx_ref, ox_ref, Lc // C)
    finish(fl_buf, gl_ref, ol_ref, S // C)


def _ret_call(dims, cq, ck, cv, cg, log_gamma, gn_g):
    B, S, Lc, R = dims
    qw = RET_HPS * RET_QK_DIM
    vw = RET_HPS * RET_V_DIM
    nhp = RET_HEADS // RET_HPS
    xoff = B * S // Lc

    def lat(w):
        return pl.BlockSpec((S, w), lambda b, h: (b, h))

    def ctx(w):
        return pl.BlockSpec((Lc, w), lambda b, h: (xoff + b, h))

    return pl.pallas_call(
        functools.partial(_ret_kernel, dims),
        grid=(B, nhp),
        in_specs=[lat(qw), ctx(qw), lat(qw), ctx(qw), lat(vw), ctx(vw), lat(vw), ctx(vw),
                  pl.BlockSpec(memory_space=pltpu.SMEM),
                  pl.BlockSpec((1, vw), lambda b, h: (0, h))],
        out_specs=[pl.BlockSpec((S, vw), lambda b, h: (b, h)),
                   pl.BlockSpec((Lc, vw), lambda b, h: (b, h))],
        out_shape=[jax.ShapeDtypeStruct((B * S, RET_HEADS * RET_V_DIM), BF16),
                   jax.ShapeDtypeStruct((B * Lc, RET_HEADS * RET_V_DIM), BF16)],
        scratch_shapes=[pltpu.VMEM((2, S, vw), F32), pltpu.VMEM((2, Lc, vw), F32)],
        compiler_params=_cparams("arbitrary", "arbitrary"),
        name="retention",
    )(cq, cq, ck, ck, cv, cv, cg, cg, log_gamma, gn_g.reshape(1, -1))


def _merge_kernel(ya_ref, yb_ref, yr_ref, yd_ref, gt_ref, x_ref, mod_ref, g_ref, wbr_ref, wo_ref, o_ref):
    acc = None
    for i, br in enumerate((ya_ref, yb_ref, yr_ref, yd_ref)):
        t = gt_ref[:, i * D_MODEL:(i + 1) * D_MODEL].astype(F32) * jnp.dot(
            br[...], wbr_ref[i], preferred_element_type=F32)
        acc = t if acc is None else acc + t
    y = jnp.dot(acc.astype(BF16), wo_ref[...], preferred_element_type=F32)
    r = y * lax.rsqrt(jnp.mean(y * y, axis=-1, keepdims=True) + EPS) * g_ref[...]
    gate = mod_ref[0][:, 2 * D_MODEL:3 * D_MODEL]
    o_ref[...] = x_ref[...] + gate * r


def _merge_call(dims, ya, yb, yr, yd, gates, xall, mod, g, w_br, w_o, tm, rows_out):
    B, S, Lc, R = dims

    def row(i):
        return (i, 0)

    br = pl.BlockSpec((tm, BRANCH_DIM), row)
    return pl.pallas_call(
        _merge_kernel,
        grid=(rows_out // tm,),
        in_specs=[br, br, br, br,
                  pl.BlockSpec((tm, N_BRANCH * D_MODEL), row),
                  pl.BlockSpec((tm, D_MODEL), row),
                  pl.BlockSpec((1, 1, 6 * D_MODEL), lambda i: (jnp.minimum(i * tm // S, B), 0, 0)),
                  pl.BlockSpec((1, D_MODEL), lambda i: (0, 0)),
                  _resident((N_BRANCH, BRANCH_DIM, D_MODEL)),
                  _resident((D_MODEL, D_MODEL))],
        out_specs=pl.BlockSpec((tm, D_MODEL), row),
        out_shape=jax.ShapeDtypeStruct((rows_out, D_MODEL), F32),
        compiler_params=_cparams("arbitrary"),
        name="merge",
    )(ya, yb, yr, yd, gates, xall, mod, g, w_br, w_o)


FFN_HALO = 8
FFN_CN = 256
FFN_LOOKAHEAD = 3


def _ffn_kernel(dims, tm, x_ref, xp_ref, xn_ref, mod_ref, g2_ref, g3_ref, wup_ref, dw_ref, db_ref, wdn_ref,
                o_ref, acc_buf):
    B, S, Lc, R = dims
    first, last = _seq_edges(pl.program_id(0), tm, B * S, S, Lc)
    m = mod_ref[0]
    g2 = g2_ref[...]
    x = x_ref[...]
    hp = jnp.where(first, 0.0, _modulated(xp_ref[...], g2, m, 1))
    hn = jnp.where(last, 0.0, _modulated(xn_ref[...], g2, m, 1))
    h = jnp.concatenate([hp, _modulated(x, g2, m, 1), hn], axis=0).astype(BF16)

    rows = tm + 2 * FFN_HALO

    def up(j):
        return tuple(jnp.dot(h, wup_ref[:, c0:c0 + FFN_CN], preferred_element_type=F32)
                     for c0 in (j * FFN_CN, D_FF + j * FFN_CN))

    def conv3(u, col0):
        w = dw_ref[:, col0:col0 + FFN_CN]
        y = (w[0:1] * pltpu.roll(u, 1, 0) + w[1:2] * u + w[2:3] * pltpu.roll(u, rows - 1, 0)
             + db_ref[:, col0:col0 + FFN_CN])
        return y[FFN_HALO:FFN_HALO + tm]

    n_chunks = D_FF // FFN_CN
    ahead = [up(j) for j in range(FFN_LOOKAHEAD)]
    for j in range(n_chunks):
        ca, cb = j * FFN_CN, D_FF + j * FFN_CN
        if j + FFN_LOOKAHEAD < n_chunks:
            ahead.append(up(j + FFN_LOOKAHEAD))
        u_cur = ahead.pop(0)
        a, b = conv3(u_cur[0], ca), conv3(u_cur[1], cb)
        t = (_silu(a) * b).astype(BF16)
        part = jnp.dot(t, wdn_ref[ca:ca + FFN_CN, :], preferred_element_type=F32)
        if j == 0:
            acc_buf[...] = part
        else:
            acc_buf[...] += part

    y = acc_buf[...]
    r = y * lax.rsqrt(jnp.mean(y * y, axis=-1, keepdims=True) + EPS) * g3_ref[...]
    o_ref[...] = x + m[:, 5 * D_MODEL:6 * D_MODEL] * r


def _ffn_call(dims, xall, mod, g2, g3, f_up, f_dw, f_dw_b, f_down, tm):
    B, S, Lc, R = dims
    rows = xall.shape[0]
    assert rows in (R, B * S)
    hb = tm // FFN_HALO
    nhb = rows // FFN_HALO
    return pl.pallas_call(
        functools.partial(_ffn_kernel, dims, tm),
        grid=(rows // tm,),
        in_specs=[
            pl.BlockSpec((tm, D_MODEL), lambda i: (i, 0)),
            pl.BlockSpec((FFN_HALO, D_MODEL), lambda i: (jnp.maximum(i * hb - 1, 0), 0)),
            pl.BlockSpec((FFN_HALO, D_MODEL), lambda i: (jnp.minimum((i + 1) * hb, nhb - 1), 0)),
            pl.BlockSpec((1, 1, 6 * D_MODEL), lambda i: (jnp.minimum(i * tm // S, B), 0, 0)),
            pl.BlockSpec((1, D_MODEL), lambda i: (0, 0)),
            pl.BlockSpec((1, D_MODEL), lambda i: (0, 0)),
            _resident((D_MODEL, 2 * D_FF)),
            pl.BlockSpec((FFN_CONV_WIDTH, 2 * D_FF), lambda i: (0, 0)),
            pl.BlockSpec((1, 2 * D_FF), lambda i: (0, 0)),
            _resident((D_FF, D_MODEL)),
        ],
        out_specs=pl.BlockSpec((tm, D_MODEL), lambda i: (i, 0)),
        out_shape=jax.ShapeDtypeStruct((rows, D_MODEL), F32),
        scratch_shapes=[pltpu.VMEM((tm, D_MODEL), F32)],
        compiler_params=_cparams("arbitrary"),
        name="conv_ffn",
    )(xall, xall, xall, mod, g2, g3, f_up, f_dw, f_dw_b.reshape(1, -1), f_down)


def _rope_tables(S, pad_rows):
    rows = S // GRID_W
    row = jnp.repeat(jnp.arange(rows), GRID_W).astype(F32)
    col = jnp.tile(jnp.arange(GRID_W), rows).astype(F32)
    half = HEAD_DIM // 2
    inv = ROPE_BASE ** (-jnp.arange(0, half, 2, dtype=F32) / half)
    ang = jnp.concatenate([row[:, None] * inv, col[:, None] * inv], axis=-1)
    cos, sin = jnp.cos(ang), jnp.sin(ang)
    zero = jnp.zeros_like(sin)
    cos_h = jnp.concatenate([cos, cos], axis=-1)
    sa_h = jnp.concatenate([-sin, zero], axis=-1)
    sb_h = jnp.concatenate([zero, sin], axis=-1)
    reps = LANES // HEAD_DIM

    def slab(t, fill):
        t = jnp.tile(t, (1, reps))
        return jnp.concatenate([t, jnp.full((pad_rows, LANES), fill, F32)], axis=0)

    return slab(cos_h, 1.0), slab(sa_h, 0.0), slab(sb_h, 0.0)


def kernel(x, c, ctx, c_ctx, ada_w, ada_b, norm_g, w_in, a_dw, a_dw_b, a_ln_g, a_ln_b, b_sink,
           c_decay_logit, c_gn_g, d_qn_g, d_kn_g, w_br, w_o, f_up, f_dw, f_dw_b, f_down):
    B, S, _ = x.shape
    Lc = ctx.shape[1]
    R = B * (S + Lc)
    dims = (B, S, Lc, R)
    tm = 256
    assert B + 1 <= MOD_ROWS and S % tm == 0 and (B * Lc) % tm == 0 and Lc % BLOCK == 0 and S % Lc == 0

    cvec = jnp.concatenate([c, c_ctx[None, :], jnp.zeros((MOD_ROWS - B - 1, D_MODEL), F32)], axis=0)
    mods = _ada_call(cvec, ada_w, ada_b).reshape(DEPTH, MOD_ROWS, 1, 6 * D_MODEL)
    cos_t, sa_t, sb_t = _rope_tables(S, tm)
    reps = LANES // HEAD_DIM
    log_gamma = jnp.log(jax.nn.sigmoid(c_decay_logit.astype(F32)))

    xall = jnp.concatenate([x.reshape(B * S, D_MODEL), ctx.reshape(B * Lc, D_MODEL)], axis=0)
    for l in range(DEPTH):
        mod = mods[l]
        ng = norm_g[l].reshape(4, 1, D_MODEL)
        (z, bq, bk, bv, cq, ck, cv, cg, dq, dk, dv, gates) = _proj_call(
            dims, xall, mod, ng[0], w_in[l].astype(BF16), cos_t, sa_t, sb_t,
            jnp.tile(d_qn_g[l], reps)[None, :], jnp.tile(d_kn_g[l], reps)[None, :], tm)
        ya = _conv_call(dims, z, a_dw[l], a_dw_b[l], a_ln_g[l], a_ln_b[l], 256)
        yb = _window_call(dims, bq, bk, bv, b_sink[l])
        yr_l, yr_x = _ret_call(dims, cq, ck, cv, cg, log_gamma[l], c_gn_g[l])
        yr = jnp.concatenate([yr_l, yr_x], axis=0)
        bound = 1.02 * HEAD_DIM * (HEAD_DIM ** -0.5) * jnp.max(jnp.abs(d_qn_g[l])) * jnp.max(jnp.abs(d_kn_g[l]))
        par = jnp.stack([bound * LOG2E, (bound <= GLB_MAX_SAFE_BOUND).astype(F32)])
        yd = _global_call(dims, par, dq, dk, dv)
        rows_out = R if l < DEPTH - 1 else B * S
        xall = _merge_call(dims, ya, yb, yr, yd, gates, xall, mod, ng[1],
                           w_br[l].astype(BF16), w_o[l].astype(BF16), tm, rows_out)
        xall = _ffn_call(dims, xall, mod, ng[2], ng[3], f_up[l].astype(BF16), f_dw[l], f_dw_b[l],
                         f_down[l].astype(BF16), tm)
    return xall.reshape(B, S, D_MODEL)
```

```python
import functools

import jax
import jax.numpy as jnp
import numpy as np
from jax import lax
from jax.experimental import pallas as pl
from jax.experimental.pallas import tpu as pltpu

D_MODEL = 1024
DEPTH = 4
GRID_W = 64
HEAD_DIM = 64
BLOCK = 128
ROPE_BASE = 10000.0
EPS = 1e-6
NEG_INF = -1e30
CONV_DIM = 512
CONV_WIDTH = 31
WIN_HEADS = 8
WIN_KV_HEADS = 2
WINDOW = 128
RET_HEADS = 4
RET_QK_DIM = 64
RET_V_DIM = 128
RET_CHUNK = 128
GLB_HEADS = 8
GLB_KV_HEADS = 2
N_BRANCH = 4
BRANCH_DIM = 512
D_FF = 2816
FFN_CONV_WIDTH = 3

IN_SIZES = (2 * CONV_DIM,
            WIN_HEADS * HEAD_DIM, WIN_KV_HEADS * HEAD_DIM, WIN_KV_HEADS * HEAD_DIM,
            RET_HEADS * RET_QK_DIM, RET_HEADS * RET_QK_DIM, RET_HEADS * RET_V_DIM, RET_HEADS * RET_V_DIM,
            GLB_HEADS * HEAD_DIM, GLB_KV_HEADS * HEAD_DIM, GLB_KV_HEADS * HEAD_DIM,
            N_BRANCH * D_MODEL)
IN_DIM = sum(IN_SIZES)
IN_OFF = tuple(int(v) for v in np.cumsum((0,) + IN_SIZES)[:-1])
(OFF_AU, OFF_BQ, OFF_BK, OFF_BV, OFF_CQ, OFF_CK, OFF_CV, OFF_CG,
 OFF_DQ, OFF_DK, OFF_DV, OFF_GATES) = IN_OFF

LANES = 128
SUBLANES = 8
MOD_ROWS = 16
VMEM_LIMIT = 56 * 1024 * 1024
F32 = jnp.float32
BF16 = jnp.bfloat16
LOG2E = 1.4426950408889634


def _sigmoid(v):
    return 1.0 / (1.0 + jnp.exp(-v))


def _silu(v):
    return v * _sigmoid(v)


def _cparams(*sem):
    return pltpu.CompilerParams(dimension_semantics=sem, vmem_limit_bytes=VMEM_LIMIT)


def _resident(shape):
    nd = len(shape)
    return pl.BlockSpec(shape, lambda *_: (0,) * nd, pipeline_mode=pl.Buffered(1))


def _ada_kernel(c_ref, w_ref, b_ref, o_ref):
    a = _silu(c_ref[...]).astype(BF16)
    w = w_ref[0].astype(BF16)
    o_ref[0] = jnp.dot(a, w, preferred_element_type=F32) + b_ref[0]


def _ada_call(cvec, ada_w, ada_b):
    tn = 1536
    n6 = 6 * D_MODEL
    return pl.pallas_call(
        _ada_kernel,
        grid=(DEPTH, n6 // tn),
        in_specs=[
            pl.BlockSpec((MOD_ROWS, D_MODEL), lambda l, j: (0, 0)),
            pl.BlockSpec((1, D_MODEL, tn), lambda l, j: (l, 0, j)),
            pl.BlockSpec((1, 1, tn), lambda l, j: (l, 0, j)),
        ],
        out_specs=pl.BlockSpec((1, MOD_ROWS, tn), lambda l, j: (l, 0, j)),
        out_shape=jax.ShapeDtypeStruct((DEPTH, MOD_ROWS, n6), F32),
        compiler_params=_cparams("arbitrary", "arbitrary"),
        name="ada_mod",
    )(cvec, ada_w, ada_b.reshape(DEPTH, 1, n6))


def _modulated(x, g, m, which):
    o = 3 * which * D_MODEL
    y = x * lax.rsqrt(jnp.mean(x * x, axis=-1, keepdims=True) + EPS) * g
    return y * (1.0 + m[:, o + D_MODEL:o + 2 * D_MODEL]) + m[:, o:o + D_MODEL]


def _rope128(v, cos, sa, sb):
    return v * cos + pltpu.roll(v, LANES - HEAD_DIM // 2, 1) * sa + pltpu.roll(v, HEAD_DIM // 2, 1) * sb


def _head_rms128(v, gain):
    lane = lax.broadcasted_iota(jnp.int32, (1, LANES), 1)
    lo = lane < HEAD_DIM
    sq = v * v
    s0 = jnp.sum(jnp.where(lo, sq, 0.0), axis=-1, keepdims=True)
    s1 = jnp.sum(jnp.where(lo, 0.0, sq), axis=-1, keepdims=True)
    ms = jnp.where(lo, s0, s1) * (1.0 / HEAD_DIM)
    return v * lax.rsqrt(ms + EPS) * gain


def _proj_kernel(x_ref, mod_ref, g_ref, w_ref, cos_ref, sa_ref, sb_ref, qn_ref, kn_ref,
                 z_ref, bq_ref, bk_ref, bv_ref, cq_ref, ck_ref, cv_ref, cg_ref,
                 dq_ref, dk_ref, dv_ref, gt_ref):
    h = _modulated(x_ref[...], g_ref[...], mod_ref[0], 0).astype(BF16)

    def proj(off, width):
        return jnp.dot(h, w_ref[:, off:off + width], preferred_element_type=F32)

    u = proj(OFF_AU, 2 * CONV_DIM)
    z_ref[...] = u[:, :CONV_DIM] * _sigmoid(u[:, CONV_DIM:])

    cos, sa, sb = cos_ref[...], sa_ref[...], sb_ref[...]
    scale = HEAD_DIM ** -0.5

    def roped(off, width, out_ref, gain_ref, mul, feature_major=False):
        v = proj(off, width)
        for c in range(width // LANES):
            s = v[:, c * LANES:(c + 1) * LANES]
            if gain_ref is not None:
                s = _head_rms128(s, gain_ref[...])
            s = _rope128(s, cos, sa, sb) * mul
            if feature_major:
                out_ref[c * LANES:(c + 1) * LANES, :] = s.T.astype(out_ref.dtype)
            else:
                out_ref[:, c * LANES:(c + 1) * LANES] = s.astype(out_ref.dtype)

    roped(OFF_BQ, WIN_HEADS * HEAD_DIM, bq_ref, None, scale * LOG2E, feature_major=True)
    roped(OFF_BK, WIN_KV_HEADS * HEAD_DIM, bk_ref, None, 1.0)
    bv_ref[...] = proj(OFF_BV, WIN_KV_HEADS * HEAD_DIM).astype(BF16).T

    cq_ref[...] = proj(OFF_CQ, RET_HEADS * RET_QK_DIM).astype(BF16)
    ck_ref[...] = (proj(OFF_CK, RET_HEADS * RET_QK_DIM) * (RET_QK_DIM ** -0.5)).astype(BF16)
    cv_ref[...] = proj(OFF_CV, RET_HEADS * RET_V_DIM).astype(BF16)
    cg_ref[...] = proj(OFF_CG, RET_HEADS * RET_V_DIM)

    roped(OFF_DQ, GLB_HEADS * HEAD_DIM, dq_ref, qn_ref, scale * LOG2E, feature_major=True)
    roped(OFF_DK, GLB_KV_HEADS * HEAD_DIM, dk_ref, kn_ref, 1.0)
    dv_ref[...] = proj(OFF_DV, GLB_KV_HEADS * HEAD_DIM).astype(BF16).T

    for i in range(N_BRANCH):
        gl = proj(OFF_GATES + i * D_MODEL, D_MODEL)
        gt_ref[:, i * D_MODEL:(i + 1) * D_MODEL] = _sigmoid(gl).astype(BF16)


def _proj_call(dims, xall, mod, g, w_in, cos_t, sa_t, sb_t, qn2, kn2, tm):
    B, S, Lc, R = dims
    nlat = B * S // tm
    n_rope_lat = S // tm

    def row(i):
        return (i, 0)

    def mod_idx(i):
        return (jnp.minimum(i * tm // S, B), 0, 0)

    def rope_idx(i):
        return (jnp.where(i < nlat, i % n_rope_lat, n_rope_lat), 0)

    outs = [(CONV_DIM, F32, False), (512, BF16, True), (128, BF16, False), (128, BF16, True),
            (256, BF16, False), (256, BF16, False), (512, BF16, False), (512, F32, False),
            (512, BF16, True), (128, BF16, False), (128, BF16, True), (4 * D_MODEL, BF16, False)]
    return pl.pallas_call(
        _proj_kernel,
        grid=(R // tm,),
        in_specs=[
            pl.BlockSpec((tm, D_MODEL), row),
            pl.BlockSpec((1, 1, 6 * D_MODEL), mod_idx),
            pl.BlockSpec((1, D_MODEL), lambda i: (0, 0)),
            _resident((D_MODEL, IN_DIM)),
            pl.BlockSpec((tm, LANES), rope_idx),
            pl.BlockSpec((tm, LANES), rope_idx),
            pl.BlockSpec((tm, LANES), rope_idx),
            pl.BlockSpec((1, LANES), lambda i: (0, 0)),
            pl.BlockSpec((1, LANES), lambda i: (0, 0)),
        ],
        out_specs=[pl.BlockSpec((w, tm), lambda i: (0, i)) if fm else pl.BlockSpec((tm, w), row)
                   for w, _, fm in outs],
        out_shape=[jax.ShapeDtypeStruct((w, R) if fm else (R, w), dt) for w, dt, fm in outs],
        compiler_params=_cparams("arbitrary"),
        name="proj",
    )(xall, mod, g, w_in, cos_t, sa_t, sb_t, qn2, kn2)


CONV_HALO = 16
CONV_ROWS = 32


def _seq_edges(i, tile, n_lat_rows, S, Lc):
    r0 = i * tile
    in_lat = r0 < n_lat_rows
    pos = jnp.where(in_lat, r0 % S, (r0 - n_lat_rows) % Lc)
    length = jnp.where(in_lat, S, Lc)
    return pos == 0, pos + tile == length


def _conv_kernel(dims, tc, z_ref, zp_ref, zn_ref, w_ref, b_ref, lg_ref, lb_ref, o_ref, buf):
    B, S, Lc, R = dims
    first, last = _seq_edges(pl.program_id(0), tc, B * S, S, Lc)
    rows = tc + 2 * CONV_HALO
    zpad = jnp.concatenate([jnp.where(first, 0.0, zp_ref[...]), z_ref[...],
                            jnp.where(last, 0.0, zn_ref[...])], axis=0)
    buf[0] = zpad
    for r in range(1, SUBLANES):
        buf[r] = pltpu.roll(zpad, rows - r, 0)
    pad = (CONV_WIDTH - 1) // 2
    bias, lg, lb = b_ref[...], lg_ref[...], lb_ref[...]

    for c in range(tc // CONV_ROWS):
        r0 = c * CONV_ROWS
        acc = jnp.zeros((CONV_ROWS, CONV_DIM), F32) + bias
        for k in range(CONV_WIDTH):
            off = CONV_HALO - pad + k
            base = r0 + (off // SUBLANES) * SUBLANES
            acc = acc + w_ref[k:k + 1, :] * buf[off % SUBLANES, base:base + CONV_ROWS, :]
        mu = jnp.mean(acc, axis=-1, keepdims=True)
        xc = acc - mu
        var = jnp.mean(xc * xc, axis=-1, keepdims=True)
        y = xc * lax.rsqrt(var + EPS) * lg + lb
        o_ref[r0:r0 + CONV_ROWS, :] = _silu(y).astype(o_ref.dtype)


def _conv_call(dims, z, a_dw, a_dw_b, a_ln_g, a_ln_b, tc):
    B, S, Lc, R = dims
    hb = tc // CONV_HALO
    nhb = R // CONV_HALO
    return pl.pallas_call(
        functools.partial(_conv_kernel, dims, tc),
        grid=(R // tc,),
        in_specs=[
            pl.BlockSpec((tc, CONV_DIM), lambda i: (i, 0)),
            pl.BlockSpec((CONV_HALO, CONV_DIM), lambda i: (jnp.maximum(i * hb - 1, 0), 0)),
            pl.BlockSpec((CONV_HALO, CONV_DIM), lambda i: (jnp.minimum((i + 1) * hb, nhb - 1), 0)),
            pl.BlockSpec((CONV_WIDTH, CONV_DIM), lambda i: (0, 0)),
            pl.BlockSpec((1, CONV_DIM), lambda i: (0, 0)),
            pl.BlockSpec((1, CONV_DIM), lambda i: (0, 0)),
            pl.BlockSpec((1, CONV_DIM), lambda i: (0, 0)),
        ],
        out_specs=pl.BlockSpec((tc, CONV_DIM), lambda i: (i, 0)),
        out_shape=jax.ShapeDtypeStruct((R, CONV_DIM), BF16),
        scratch_shapes=[pltpu.VMEM((SUBLANES, tc + 2 * CONV_HALO, CONV_DIM), F32)],
        compiler_params=_cparams("arbitrary"),
        name="conv_branch",
    )(z, z, z, a_dw, a_dw_b.reshape(1, -1), a_ln_g.reshape(1, -1), a_ln_b.reshape(1, -1))


def _stacked_q_weights(qT, kv_heads, grp):
    zeros = jnp.zeros((HEAD_DIM, grp * BLOCK), BF16)
    out = []
    for kvh in range(kv_heads):
        piece = jnp.concatenate([qT[(kvh * grp + g) * HEAD_DIM:(kvh * grp + g + 1) * HEAD_DIM, :]
                                 for g in range(grp)], axis=1)
        out.append(jnp.concatenate([piece, zeros] if kvh == 0 else [zeros, piece], axis=0))
    return out


def _unstack_heads(oT, kv_heads, grp):
    OT = jnp.concatenate([oT[kvh][:, g * BLOCK:(g + 1) * BLOCK]
                          for kvh in range(kv_heads) for g in range(grp)], axis=0)
    return OT.T


def _window_kernel(dims, qT_ref, kp_ref, kc_ref, kn_ref, kx_ref, vp_ref, vc_ref, vn_ref, vx_ref,
                   sink_ref, o_ref):
    B, S, Lc, R = dims
    nqb = S // BLOCK
    j = pl.program_id(1)
    is_lat = j < nqb
    grp = WIN_HEADS // WIN_KV_HEADS
    nk = 3 * BLOCK + Lc

    key = lax.broadcasted_iota(jnp.int32, (nk, BLOCK), 0)
    qry = lax.broadcasted_iota(jnp.int32, (nk, BLOCK), 1)
    d = key - qry
    lo = jnp.where(j >= 1, 0, BLOCK)
    hi = jnp.where(is_lat, jnp.where(j <= nqb - 2, 3 * BLOCK, 2 * BLOCK), 0)
    band = (d >= BLOCK - WINDOW) & (d <= BLOCK + WINDOW) & (key >= lo) & (key < hi)
    bias = jnp.where(band | (key >= 3 * BLOCK), 0.0, NEG_INF)
    bias = jnp.concatenate([bias] * grp, axis=1)

    W = _stacked_q_weights(qT_ref[...], WIN_KV_HEADS, grp)
    kk = jnp.concatenate([kp_ref[...], kc_ref[...], kn_ref[...], kx_ref[...]], axis=0)
    vT = jnp.concatenate([vp_ref[...], vc_ref[...], vn_ref[...], vx_ref[...]], axis=1)
    oT = []
    scores = [jnp.dot(kk, W[kvh], preferred_element_type=F32) for kvh in range(WIN_KV_HEADS)]
    probs = []
    for kvh in range(WIN_KV_HEADS):
        sk = jnp.concatenate([jnp.full((1, BLOCK), sink_ref[kvh * grp + g] * LOG2E, F32)
                              for g in range(grp)], axis=1)
        sT = scores[kvh] + bias
        m = jnp.maximum(jnp.max(sT, axis=0, keepdims=True), sk)
        pT = jnp.exp2(sT - m)
        l = jnp.sum(pT, axis=0, keepdims=True) + jnp.exp2(sk - m)
        probs.append((pT.astype(BF16), 1.0 / l))
    for kvh, (pT, inv_l) in enumerate(probs):
        oT.append(jnp.dot(vT[kvh * HEAD_DIM:(kvh + 1) * HEAD_DIM, :], pT, preferred_element_type=F32) * inv_l)
    o_ref[...] = _unstack_heads(oT, WIN_KV_HEADS, grp).astype(o_ref.dtype)


def _window_call(dims, bqT, bk, bvT, sink, ctx_queries):
    B, S, Lc, R = dims
    nqb, ncb = S // BLOCK, Lc // BLOCK
    n_steps = nqb + (ncb if ctx_queries else 0)
    lat_blocks = B * nqb
    kvw = WIN_KV_HEADS * HEAD_DIM
    qw = WIN_HEADS * HEAD_DIM

    def q_blk(b, j):
        return jnp.where(j < nqb, b * nqb + j, lat_blocks + b * ncb + (j - nqb))

    def k_blk(b, j, delta):
        return b * nqb + jnp.clip(j + delta, 0, nqb - 1)

    kspec = [pl.BlockSpec((BLOCK, kvw), lambda b, j, d=d: (k_blk(b, j, d), 0)) for d in (-1, 0, 1)]
    kspec.append(pl.BlockSpec((Lc, kvw), lambda b, j: (B * S // Lc + b, 0)))
    vspec = [pl.BlockSpec((kvw, BLOCK), lambda b, j, d=d: (0, k_blk(b, j, d))) for d in (-1, 0, 1)]
    vspec.append(pl.BlockSpec((kvw, Lc), lambda b, j: (0, B * S // Lc + b)))
    return pl.pallas_call(
        functools.partial(_window_kernel, dims),
        grid=(B, n_steps),
        in_specs=[pl.BlockSpec((qw, BLOCK), lambda b, j: (0, q_blk(b, j)))] + kspec + vspec
        + [pl.BlockSpec(memory_space=pltpu.SMEM)],
        out_specs=pl.BlockSpec((BLOCK, qw), lambda b, j: (q_blk(b, j), 0)),
        out_shape=jax.ShapeDtypeStruct((R, qw), BF16),
        compiler_params=_cparams("arbitrary", "arbitrary"),
        name="window_attn",
    )(bqT, bk, bk, bk, bk, bvT, bvT, bvT, bvT, sink)


GLB_TK = 128
GLB_ONLINE_TK = 256
GLB_MAX_SAFE_BOUND = 40.0


def _global_kernel(dims, par_ref, qT_ref, kl_ref, kx_ref, vTl_ref, vTx_ref, o_ref):
    B, S, Lc, R = dims
    nqb = S // BLOCK
    is_lat = pl.program_id(1) < nqb
    grp = GLB_HEADS // GLB_KV_HEADS
    cols = grp * BLOCK
    shift = par_ref[0]
    W = _stacked_q_weights(qT_ref[...], GLB_KV_HEADS, grp)

    def scores(kk):
        return tuple(jnp.dot(kk, W[kvh], preferred_element_type=F32) for kvh in range(GLB_KV_HEADS))

    def v_rows(vT, kvh):
        return vT[kvh * HEAD_DIM:(kvh + 1) * HEAD_DIM, :]

    def bounded(chunks):
        ls = [jnp.zeros((1, cols), F32)] * GLB_KV_HEADS
        accs = [jnp.zeros((HEAD_DIM, cols), F32)] * GLB_KV_HEADS
        s_cur = scores(chunks[0][0][chunks[0][2]:chunks[0][2] + GLB_TK, :])
        for i, (k_ref, v_ref, r0) in enumerate(chunks):
            s_next = None
            if i + 1 < len(chunks):
                nk, _, nr = chunks[i + 1]
                s_next = scores(nk[nr:nr + GLB_TK, :])
            vT = v_ref[:, r0:r0 + GLB_TK]
            for kvh in range(GLB_KV_HEADS):
                pT = jnp.exp2(s_cur[kvh] - shift)
                ls[kvh] = ls[kvh] + jnp.sum(pT, axis=0, keepdims=True)
                accs[kvh] = accs[kvh] + jnp.dot(v_rows(vT, kvh), pT.astype(BF16), preferred_element_type=F32)
            s_cur = s_next
        return tuple(acc * (1.0 / l) for l, acc in zip(ls, accs))

    def online():
        def update(carry, kk, vT):
            new = []
            for kvh, sT in enumerate(scores(kk)):
                m, l, acc = carry[kvh]
                m_new = jnp.maximum(m, jnp.max(sT, axis=0, keepdims=True))
                alpha = jnp.exp2(m - m_new)
                pT = jnp.exp2(sT - m_new)
                l = alpha * l + jnp.sum(pT, axis=0, keepdims=True)
                acc = alpha * acc + jnp.dot(v_rows(vT, kvh), pT.astype(BF16), preferred_element_type=F32)
                new.append((m_new, l, acc))
            return tuple(new)

        init = tuple((jnp.full((1, cols), NEG_INF, F32), jnp.zeros((1, cols), F32),
                      jnp.zeros((HEAD_DIM, cols), F32)) for _ in range(GLB_KV_HEADS))
        carry = init
        for r0 in range(0, Lc, GLB_ONLINE_TK):
            carry = update(carry, kx_ref[r0:r0 + GLB_ONLINE_TK, :], vTx_ref[:, r0:r0 + GLB_ONLINE_TK])

        def body(c, carry):
            r0 = pl.multiple_of(c * GLB_ONLINE_TK, GLB_ONLINE_TK)
            return update(carry, kl_ref[pl.ds(r0, GLB_ONLINE_TK), :], vTl_ref[:, pl.ds(r0, GLB_ONLINE_TK)])

        carry = lax.fori_loop(0, jnp.where(is_lat, S // GLB_ONLINE_TK, 0), body, carry)
        return tuple(acc * (1.0 / l) for (m, l, acc) in carry)

    ctx_chunks = [(kx_ref, vTx_ref, r0) for r0 in range(0, Lc, GLB_TK)]
    lat_chunks = [(kl_ref, vTl_ref, r0) for r0 in range(0, S, GLB_TK)]

    def bounded_any():
        return lax.cond(is_lat, lambda: bounded(ctx_chunks + lat_chunks), lambda: bounded(ctx_chunks))

    oT = lax.cond(par_ref[1] > 0.5, bounded_any, online)
    o_ref[...] = _unstack_heads(oT, GLB_KV_HEADS, grp).astype(o_ref.dtype)


def _global_call(dims, par, dqT, dk, dvT, ctx_queries):
    B, S, Lc, R = dims
    nqb, ncb = S // BLOCK, Lc // BLOCK
    n_steps = nqb + (ncb if ctx_queries else 0)
    lat_blocks = B * nqb
    kvw = GLB_KV_HEADS * HEAD_DIM
    qw = GLB_HEADS * HEAD_DIM
    assert Lc % GLB_ONLINE_TK == 0 and S % GLB_ONLINE_TK == 0 and GLB_ONLINE_TK % GLB_TK == 0

    def q_blk(b, j):
        return jnp.where(j < nqb, b * nqb + j, lat_blocks + b * ncb + (j - nqb))

    return pl.pallas_call(
        functools.partial(_global_kernel, dims),
        grid=(B, n_steps),
        in_specs=[pl.BlockSpec(memory_space=pltpu.SMEM),
                  pl.BlockSpec((qw, BLOCK), lambda b, j: (0, q_blk(b, j))),
                  pl.BlockSpec((S, kvw), lambda b, j: (b, 0)),
                  pl.BlockSpec((Lc, kvw), lambda b, j: (B * S // Lc + b, 0)),
                  pl.BlockSpec((kvw, S), lambda b, j: (0, b)),
                  pl.BlockSpec((kvw, Lc), lambda b, j: (0, B * S // Lc + b))],
        out_specs=pl.BlockSpec((BLOCK, qw), lambda b, j: (q_blk(b, j), 0)),
        out_shape=jax.ShapeDtypeStruct((R, qw), BF16),
        compiler_params=_cparams("arbitrary", "arbitrary"),
        name="global_attn",
    )(par, dqT, dk, dk, dvT, dvT)


RET_HPS = 2
RET_KERNEL_CHUNK = 256


def _ret_kernel(dims, ql_ref, qx_ref, kl_ref, kx_ref, vl_ref, vx_ref, gl_ref, gx_ref, lg_ref, gn_ref,
                ol_ref, ox_ref, fl_buf, fx_buf):
    B, S, Lc, R = dims
    C = RET_KERNEL_CHUNK
    assert S % C == 0 and Lc % C == 0
    hp = pl.program_id(1)
    pos_r = lax.broadcasted_iota(jnp.int32, (C, C), 0).astype(F32)
    pos_c = lax.broadcasted_iota(jnp.int32, (C, C), 1).astype(F32)
    diff = pos_r - pos_c
    col_pos = lax.broadcasted_iota(jnp.int32, (C, 1), 0).astype(F32)

    tabs = {}
    for dr in range(2):
        for hh in range(RET_HPS):
            lg = lg_ref[dr, hp * RET_HPS + hh]
            dd = diff if dr == 0 else -diff
            dmat = jnp.where(dd >= 0, jnp.exp(lg * jnp.maximum(dd, 0.0)), 0.0)
            p_eff = col_pos if dr == 0 else (C - 1.0) - col_pos
            kdec = jnp.exp(lg * ((C - 1.0) - p_eff))
            qdec = jnp.exp(lg * (p_eff + 1.0))
            cdec = jnp.exp(jnp.zeros((1, RET_V_DIM), F32) + lg * float(C))
            tabs[dr, hh] = (dmat, kdec, qdec, cdec)

    qk_w = RET_HPS * RET_QK_DIM
    lane = lax.broadcasted_iota(jnp.int32, (1, qk_w), 1)
    row = lax.broadcasted_iota(jnp.int32, (qk_w, 1), 0)
    lane_m = [jnp.where((lane >= hh * RET_QK_DIM) & (lane < (hh + 1) * RET_QK_DIM), 1.0, 0.0)
              for hh in range(RET_HPS)]
    row_m = [jnp.where((row >= hh * RET_QK_DIM) & (row < (hh + 1) * RET_QK_DIM), 1.0, 0.0)
             for hh in range(RET_HPS)]

    def step(q_ref, k_ref, v_ref, r0, dr, hh, state):
        dmat, kdec, qdec, cdec = tabs[dr, hh]
        vs = slice(hh * RET_V_DIM, (hh + 1) * RET_V_DIM)
        qf = q_ref[pl.ds(r0, C), :].astype(F32) * lane_m[hh]
        k2 = k_ref[pl.ds(r0, C), :]
        vc = v_ref[pl.ds(r0, C), vs]
        sc = lax.dot_general(qf.astype(BF16), k2, (((1,), (1,)), ((), ())), preferred_element_type=F32) * dmat
        inner = jnp.dot(sc.astype(BF16), vc, preferred_element_type=F32)
        qd = (qf * qdec).astype(BF16)
        cross = jnp.dot(qd, state.astype(BF16), preferred_element_type=F32)
        kd = (k2.astype(F32) * kdec).astype(BF16)
        u = lax.dot_general(kd, vc, (((0,), (0,)), ((), ())), preferred_element_type=F32)
        return inner + cross, cdec * state + u * row_m[hh]

    def sweep(q_ref, k_ref, v_ref, f_buf, n_chunks, states):
        def body(i, st):
            st = list(st)
            rf = pl.multiple_of(i * C, C)
            rb = pl.multiple_of((n_chunks - 1 - i) * C, C)
            for hh in range(RET_HPS):
                vs = slice(hh * RET_V_DIM, (hh + 1) * RET_V_DIM)
                of, st[hh] = step(q_ref, k_ref, v_ref, rf, 0, hh, st[hh])
                f_buf[0, pl.ds(rf, C), vs] = of
                ob, st[RET_HPS + hh] = step(q_ref, k_ref, v_ref, rb, 1, hh, st[RET_HPS + hh])
                f_buf[1, pl.ds(rb, C), vs] = ob
            return tuple(st)
        return lax.fori_loop(0, n_chunks, body, states)

    def finish(f_buf, g_ref, o_ref, n_chunks):
        def body(i, carry):
            r0 = pl.multiple_of(i * C, C)
            for hh in range(RET_HPS):
                vs = slice(hh * RET_V_DIM, (hh + 1) * RET_V_DIM)
                o = f_buf[0, pl.ds(r0, C), vs] + f_buf[1, pl.ds(r0, C), vs]
                mu = jnp.mean(o, axis=-1, keepdims=True)
                xc = o - mu
                var = jnp.mean(xc * xc, axis=-1, keepdims=True)
                y = xc * lax.rsqrt(var + EPS) * gn_ref[:, vs]
                o_ref[pl.ds(r0, C), vs] = (_silu(g_ref[pl.ds(r0, C), vs]) * y).astype(o_ref.dtype)
            return carry
        lax.fori_loop(0, n_chunks, body, 0)

    zero = jnp.zeros((qk_w, RET_V_DIM), F32)
    states = sweep(qx_ref, kx_ref, vx_ref, fx_buf, Lc // C, (zero,) * (2 * RET_HPS))
    sweep(ql_ref, kl_ref, vl_ref, fl_buf, S // C, states)
    finish(fx_buf, gx_ref, ox_ref, Lc // C)
    finish(fl_buf, gl_ref, ol_ref, S // C)


def _ret_call(dims, cq, ck, cv, cg, log_gamma, gn_g):
    B, S, Lc, R = dims
    qw = RET_HPS * RET_QK_DIM
    vw = RET_HPS * RET_V_DIM
    nhp = RET_HEADS // RET_HPS
    xoff = B * S // Lc

    def lat(w):
        return pl.BlockSpec((S, w), lambda b, h: (b, h))

    def ctx(w):
        return pl.BlockSpec((Lc, w), lambda b, h: (xoff + b, h))

    return pl.pallas_call(
        functools.partial(_ret_kernel, dims),
        grid=(B, nhp),
        in_specs=[lat(qw), ctx(qw), lat(qw), ctx(qw), lat(vw), ctx(vw), lat(vw), ctx(vw),
                  pl.BlockSpec(memory_space=pltpu.SMEM),
                  pl.BlockSpec((1, vw), lambda b, h: (0, h))],
        out_specs=[pl.BlockSpec((S, vw), lambda b, h: (b, h)),
                   pl.BlockSpec((Lc, vw), lambda b, h: (b, h))],
        out_shape=[jax.ShapeDtypeStruct((B * S, RET_HEADS * RET_V_DIM), BF16),
                   jax.ShapeDtypeStruct((B * Lc, RET_HEADS * RET_V_DIM), BF16)],
        scratch_shapes=[pltpu.VMEM((2, S, vw), F32), pltpu.VMEM((2, Lc, vw), F32)],
        compiler_params=_cparams("arbitrary", "arbitrary"),
        name="retention",
    )(cq, cq, ck, ck, cv, cv, cg, cg, log_gamma, gn_g.reshape(1, -1))


MERGE_SUB = 256


def _merge_kernel(n_lat_tiles, ya_ref, yb_ref, yrl_ref, yrx_ref, yd_ref, gt_ref, x_ref, mod_ref, g_ref,
                  wbr_ref, wo_ref, o_ref):
    tm = x_ref.shape[0]
    is_lat = pl.program_id(0) < n_lat_tiles
    gate = mod_ref[0][:, 2 * D_MODEL:3 * D_MODEL]
    subs = [slice(r0, r0 + MERGE_SUB) for r0 in range(0, tm, MERGE_SUB)]

    def branch_dots(rs):
        yr = jnp.where(is_lat, yrl_ref[rs, :], yrx_ref[rs, :])
        return [jnp.dot(b, wbr_ref[i], preferred_element_type=F32)
                for i, b in enumerate((ya_ref[rs, :], yb_ref[rs, :], yr, yd_ref[rs, :]))]

    dots = [branch_dots(rs) for rs in subs]
    for rs, d in zip(subs, dots):
        acc = None
        for i in range(N_BRANCH):
            t = gt_ref[rs, i * D_MODEL:(i + 1) * D_MODEL].astype(F32) * d[i]
            acc = t if acc is None else acc + t
        y = jnp.dot(acc.astype(BF16), wo_ref[...], preferred_element_type=F32)
        r = y * lax.rsqrt(jnp.mean(y * y, axis=-1, keepdims=True) + EPS) * g_ref[...]
        o_ref[rs, :] = x_ref[rs, :] + gate * r


def _merge_call(dims, ya, yb, yr_l, yr_x, yd, gates, xall, mod, g, w_br, w_o, tm, rows_out):
    B, S, Lc, R = dims
    assert tm % MERGE_SUB == 0
    n_lat = B * S // tm
    n_ctx = B * Lc // tm

    def row(i):
        return (i, 0)

    br = pl.BlockSpec((tm, BRANCH_DIM), row)
    return pl.pallas_call(
        functools.partial(_merge_kernel, n_lat),
        grid=(rows_out // tm,),
        in_specs=[br, br,
                  pl.BlockSpec((tm, BRANCH_DIM), lambda i: (jnp.minimum(i, n_lat - 1), 0)),
                  pl.BlockSpec((tm, BRANCH_DIM), lambda i: (jnp.clip(i - n_lat, 0, n_ctx - 1), 0)),
                  br,
                  pl.BlockSpec((tm, N_BRANCH * D_MODEL), row),
                  pl.BlockSpec((tm, D_MODEL), row),
                  pl.BlockSpec((1, 1, 6 * D_MODEL), lambda i: (jnp.minimum(i * tm // S, B), 0, 0)),
                  pl.BlockSpec((1, D_MODEL), lambda i: (0, 0)),
                  _resident((N_BRANCH, BRANCH_DIM, D_MODEL)),
                  _resident((D_MODEL, D_MODEL))],
        out_specs=pl.BlockSpec((tm, D_MODEL), row),
        out_shape=jax.ShapeDtypeStruct((rows_out, D_MODEL), F32),
        compiler_params=_cparams("arbitrary"),
        name="merge",
    )(ya, yb, yr_l, yr_x, yd, gates, xall, mod, g, w_br, w_o)


FFN_HALO = 8
FFN_CN = 256
FFN_LOOKAHEAD = 3


def _ffn_kernel(dims, tm, x_ref, xp_ref, xn_ref, mod_ref, g2_ref, g3_ref, wup_ref, dw_ref, db_ref, wdn_ref,
                o_ref, acc_buf):
    B, S, Lc, R = dims
    first, last = _seq_edges(pl.program_id(0), tm, B * S, S, Lc)
    m = mod_ref[0]
    g2 = g2_ref[...]
    x = x_ref[...]
    hp = jnp.where(first, 0.0, _modulated(xp_ref[...], g2, m, 1))
    hn = jnp.where(last, 0.0, _modulated(xn_ref[...], g2, m, 1))
    h = jnp.concatenate([hp, _modulated(x, g2, m, 1), hn], axis=0).astype(BF16)

    rows = tm + 2 * FFN_HALO

    def up(j):
        return tuple(jnp.dot(h, wup_ref[:, c0:c0 + FFN_CN], preferred_element_type=F32)
                     for c0 in (j * FFN_CN, D_FF + j * FFN_CN))

    def conv3(u, col0):
        w = dw_ref[:, col0:col0 + FFN_CN]
        y = (w[0:1] * pltpu.roll(u, 1, 0) + w[1:2] * u + w[2:3] * pltpu.roll(u, rows - 1, 0)
             + db_ref[:, col0:col0 + FFN_CN])
        return y[FFN_HALO:FFN_HALO + tm]

    n_chunks = D_FF // FFN_CN
    ahead = [up(j) for j in range(FFN_LOOKAHEAD)]
    for j in range(n_chunks):
        ca, cb = j * FFN_CN, D_FF + j * FFN_CN
        if j + FFN_LOOKAHEAD < n_chunks:
            ahead.append(up(j + FFN_LOOKAHEAD))
        u_cur = ahead.pop(0)
        a, b = conv3(u_cur[0], ca), conv3(u_cur[1], cb)
        t = (_silu(a) * b).astype(BF16)
        part = jnp.dot(t, wdn_ref[ca:ca + FFN_CN, :], preferred_element_type=F32)
        if j == 0:
            acc_buf[...] = part
        else:
            acc_buf[...] += part

    y = acc_buf[...]
    r = y * lax.rsqrt(jnp.mean(y * y, axis=-1, keepdims=True) + EPS) * g3_ref[...]
    o_ref[...] = x + m[:, 5 * D_MODEL:6 * D_MODEL] * r


def _ffn_call(dims, xall, mod, g2, g3, f_up, f_dw, f_dw_b, f_down, tm):
    B, S, Lc, R = dims
    rows = xall.shape[0]
    assert rows in (R, B * S)
    hb = tm // FFN_HALO
    nhb = rows // FFN_HALO
    return pl.pallas_call(
        functools.partial(_ffn_kernel, dims, tm),
        grid=(rows // tm,),
        in_specs=[
            pl.BlockSpec((tm, D_MODEL), lambda i: (i, 0)),
            pl.BlockSpec((FFN_HALO, D_MODEL), lambda i: (jnp.maximum(i * hb - 1, 0), 0)),
            pl.BlockSpec((FFN_HALO, D_MODEL), lambda i: (jnp.minimum((i + 1) * hb, nhb - 1), 0)),
            pl.BlockSpec((1, 1, 6 * D_MODEL), lambda i: (jnp.minimum(i * tm // S, B), 0, 0)),
            pl.BlockSpec((1, D_MODEL), lambda i: (0, 0)),
            pl.BlockSpec((1, D_MODEL), lambda i: (0, 0)),
            _resident((D_MODEL, 2 * D_FF)),
            pl.BlockSpec((FFN_CONV_WIDTH, 2 * D_FF), lambda i: (0, 0)),
            pl.BlockSpec((1, 2 * D_FF), lambda i: (0, 0)),
            _resident((D_FF, D_MODEL)),
        ],
        out_specs=pl.BlockSpec((tm, D_MODEL), lambda i: (i, 0)),
        out_shape=jax.ShapeDtypeStruct((rows, D_MODEL), F32),
        scratch_shapes=[pltpu.VMEM((tm, D_MODEL), F32)],
        compiler_params=_cparams("arbitrary"),
        name="conv_ffn",
    )(xall, xall, xall, mod, g2, g3, f_up, f_dw, f_dw_b.reshape(1, -1), f_down)


def _rope_tables(S, pad_rows):
    rows = S // GRID_W
    row = jnp.repeat(jnp.arange(rows), GRID_W).astype(F32)
    col = jnp.tile(jnp.arange(GRID_W), rows).astype(F32)
    half = HEAD_DIM // 2
    inv = ROPE_BASE ** (-jnp.arange(0, half, 2, dtype=F32) / half)
    ang = jnp.concatenate([row[:, None] * inv, col[:, None] * inv], axis=-1)
    cos, sin = jnp.cos(ang), jnp.sin(ang)
    zero = jnp.zeros_like(sin)
    cos_h = jnp.concatenate([cos, cos], axis=-1)
    sa_h = jnp.concatenate([-sin, zero], axis=-1)
    sb_h = jnp.concatenate([zero, sin], axis=-1)
    reps = LANES // HEAD_DIM

    def slab(t, fill):
        t = jnp.tile(t, (1, reps))
        return jnp.concatenate([t, jnp.full((pad_rows, LANES), fill, F32)], axis=0)

    return slab(cos_h, 1.0), slab(sa_h, 0.0), slab(sb_h, 0.0)


def kernel(x, c, ctx, c_ctx, ada_w, ada_b, norm_g, w_in, a_dw, a_dw_b, a_ln_g, a_ln_b, b_sink,
           c_decay_logit, c_gn_g, d_qn_g, d_kn_g, w_br, w_o, f_up, f_dw, f_dw_b, f_down):
    B, S, _ = x.shape
    Lc = ctx.shape[1]
    R = B * (S + Lc)
    dims = (B, S, Lc, R)
    tm = 256
    assert B + 1 <= MOD_ROWS and S % tm == 0 and (B * Lc) % tm == 0 and Lc % BLOCK == 0 and S % Lc == 0

    cvec = jnp.concatenate([c, c_ctx[None, :], jnp.zeros((MOD_ROWS - B - 1, D_MODEL), F32)], axis=0)
    mods = _ada_call(cvec, ada_w, ada_b).reshape(DEPTH, MOD_ROWS, 1, 6 * D_MODEL)
    cos_t, sa_t, sb_t = _rope_tables(S, tm)
    reps = LANES // HEAD_DIM
    log_gamma = jnp.log(jax.nn.sigmoid(c_decay_logit.astype(F32)))

    xall = jnp.concatenate([x.reshape(B * S, D_MODEL), ctx.reshape(B * Lc, D_MODEL)], axis=0)
    for l in range(DEPTH):
        mod = mods[l]
        ng = norm_g[l].reshape(4, 1, D_MODEL)
        (z, bq, bk, bv, cq, ck, cv, cg, dq, dk, dv, gates) = _proj_call(
            dims, xall, mod, ng[0], w_in[l].astype(BF16), cos_t, sa_t, sb_t,
            jnp.tile(d_qn_g[l], reps)[None, :], jnp.tile(d_kn_g[l], reps)[None, :], tm)
        with_ctx = l < DEPTH - 1
        ya = _conv_call(dims, z, a_dw[l], a_dw_b[l], a_ln_g[l], a_ln_b[l], 256)
        yb = _window_call(dims, bq, bk, bv, b_sink[l], with_ctx)
        yr_l, yr_x = _ret_call(dims, cq, ck, cv, cg, log_gamma[l], c_gn_g[l])
        bound = 1.02 * HEAD_DIM * (HEAD_DIM ** -0.5) * jnp.max(jnp.abs(d_qn_g[l])) * jnp.max(jnp.abs(d_kn_g[l]))
        par = jnp.stack([bound * LOG2E, (bound <= GLB_MAX_SAFE_BOUND).astype(F32)])
        yd = _global_call(dims, par, dq, dk, dv, with_ctx)
        rows_out = R if with_ctx else B * S
        xall = _merge_call(dims, ya, yb, yr_l, yr_x, yd, gates, xall, mod, ng[1],
                           w_br[l].astype(BF16), w_o[l].astype(BF16), 2 * MERGE_SUB, rows_out)
        xall = _ffn_call(dims, xall, mod, ng[2], ng[3], f_up[l].astype(BF16), f_dw[l], f_dw_b[l],
                         f_down[l].astype(BF16), tm)
    return xall.reshape(B, S, D_MODEL)
```

```python
import functools

import jax
import jax.numpy as jnp
import numpy as np
from jax import lax
from jax.experimental import pallas as pl
from jax.experimental.pallas import tpu as pltpu

D_MODEL = 1024
DEPTH = 4
GRID_W = 64
HEAD_DIM = 64
BLOCK = 128
ROPE_BASE = 10000.0
EPS = 1e-6
NEG_INF = -1e30
CONV_DIM = 512
CONV_WIDTH = 31
WIN_HEADS = 8
WIN_KV_HEADS = 2
WINDOW = 128
RET_HEADS = 4
RET_QK_DIM = 64
RET_V_DIM = 128
RET_CHUNK = 128
GLB_HEADS = 8
GLB_KV_HEADS = 2
N_BRANCH = 4
BRANCH_DIM = 512
D_FF = 2816
FFN_CONV_WIDTH = 3

IN_SIZES = (2 * CONV_DIM,
            WIN_HEADS * HEAD_DIM, WIN_KV_HEADS * HEAD_DIM, WIN_KV_HEADS * HEAD_DIM,
            RET_HEADS * RET_QK_DIM, RET_HEADS * RET_QK_DIM, RET_HEADS * RET_V_DIM, RET_HEADS * RET_V_DIM,
            GLB_HEADS * HEAD_DIM, GLB_KV_HEADS * HEAD_DIM, GLB_KV_HEADS * HEAD_DIM,
            N_BRANCH * D_MODEL)
IN_DIM = sum(IN_SIZES)
IN_OFF = tuple(int(v) for v in np.cumsum((0,) + IN_SIZES)[:-1])
(OFF_AU, OFF_BQ, OFF_BK, OFF_BV, OFF_CQ, OFF_CK, OFF_CV, OFF_CG,
 OFF_DQ, OFF_DK, OFF_DV, OFF_GATES) = IN_OFF

LANES = 128
SUBLANES = 8
MOD_ROWS = 16
ROW_TILE = 256
VMEM_LIMIT = 56 * 1024 * 1024
F32 = jnp.float32
BF16 = jnp.bfloat16
LOG2E = 1.4426950408889634


def _sigmoid(v):
    return 1.0 / (1.0 + jnp.exp(-v))


def _silu(v):
    return v * _sigmoid(v)


def _cparams(*sem):
    return pltpu.CompilerParams(dimension_semantics=sem, vmem_limit_bytes=VMEM_LIMIT)


def _resident(shape):
    nd = len(shape)
    return pl.BlockSpec(shape, lambda *_: (0,) * nd, pipeline_mode=pl.Buffered(1))


def _ada_kernel(c_ref, w_ref, b_ref, o_ref):
    a = _silu(c_ref[...]).astype(BF16)
    w = w_ref[0].astype(BF16)
    o_ref[0] = jnp.dot(a, w, preferred_element_type=F32) + b_ref[0]


def _ada_call(cvec, ada_w, ada_b):
    tn = 1536
    n6 = 6 * D_MODEL
    return pl.pallas_call(
        _ada_kernel,
        grid=(DEPTH, n6 // tn),
        in_specs=[
            pl.BlockSpec((MOD_ROWS, D_MODEL), lambda l, j: (0, 0)),
            pl.BlockSpec((1, D_MODEL, tn), lambda l, j: (l, 0, j)),
            pl.BlockSpec((1, 1, tn), lambda l, j: (l, 0, j)),
        ],
        out_specs=pl.BlockSpec((1, MOD_ROWS, tn), lambda l, j: (l, 0, j)),
        out_shape=jax.ShapeDtypeStruct((DEPTH, MOD_ROWS, n6), F32),
        compiler_params=_cparams("arbitrary", "arbitrary"),
        name="ada_mod",
    )(cvec, ada_w, ada_b.reshape(DEPTH, 1, n6))


def _modulated(x, g, m, which):
    o = 3 * which * D_MODEL
    y = x * lax.rsqrt(jnp.mean(x * x, axis=-1, keepdims=True) + EPS) * g
    return y * (1.0 + m[:, o + D_MODEL:o + 2 * D_MODEL]) + m[:, o:o + D_MODEL]


def _rope128(v, cos, sa, sb):
    return v * cos + pltpu.roll(v, LANES - HEAD_DIM // 2, 1) * sa + pltpu.roll(v, HEAD_DIM // 2, 1) * sb


def _head_rms128(v, gain):
    lane = lax.broadcasted_iota(jnp.int32, (1, LANES), 1)
    lo = lane < HEAD_DIM
    sq = v * v
    s0 = jnp.sum(jnp.where(lo, sq, 0.0), axis=-1, keepdims=True)
    s1 = jnp.sum(jnp.where(lo, 0.0, sq), axis=-1, keepdims=True)
    ms = jnp.where(lo, s0, s1) * (1.0 / HEAD_DIM)
    return v * lax.rsqrt(ms + EPS) * gain


def _proj_kernel(x_ref, mod_ref, g_ref, w_ref, cos_ref, sa_ref, sb_ref, qn_ref, kn_ref,
                 z_ref, bq_ref, bk_ref, bv_ref, cq_ref, ck_ref, cv_ref, cg_ref,
                 dq_ref, dk_ref, dv_ref, gt_ref):
    h = _modulated(x_ref[...], g_ref[...], mod_ref[0], 0).astype(BF16)

    def proj(off, width):
        return jnp.dot(h, w_ref[:, off:off + width], preferred_element_type=F32)

    u = proj(OFF_AU, 2 * CONV_DIM)
    z_ref[...] = u[:, :CONV_DIM] * _sigmoid(u[:, CONV_DIM:])

    cos, sa, sb = cos_ref[...], sa_ref[...], sb_ref[...]
    scale = HEAD_DIM ** -0.5

    def roped(off, width, out_ref, gain_ref, mul, feature_major=False):
        v = proj(off, width)
        for c in range(width // LANES):
            s = v[:, c * LANES:(c + 1) * LANES]
            if gain_ref is not None:
                s = _head_rms128(s, gain_ref[...])
            s = _rope128(s, cos, sa, sb) * mul
            if feature_major:
                out_ref[c * LANES:(c + 1) * LANES, :] = s.T.astype(out_ref.dtype)
            else:
                out_ref[:, c * LANES:(c + 1) * LANES] = s.astype(out_ref.dtype)

    roped(OFF_BQ, WIN_HEADS * HEAD_DIM, bq_ref, None, scale * LOG2E, feature_major=True)
    roped(OFF_BK, WIN_KV_HEADS * HEAD_DIM, bk_ref, None, 1.0)
    bv_ref[...] = proj(OFF_BV, WIN_KV_HEADS * HEAD_DIM).astype(BF16).T

    cq_ref[...] = proj(OFF_CQ, RET_HEADS * RET_QK_DIM).astype(BF16)
    ck_ref[...] = (proj(OFF_CK, RET_HEADS * RET_QK_DIM) * (RET_QK_DIM ** -0.5)).astype(BF16)
    cv_ref[...] = proj(OFF_CV, RET_HEADS * RET_V_DIM).astype(BF16)
    cg_ref[...] = proj(OFF_CG, RET_HEADS * RET_V_DIM)

    roped(OFF_DQ, GLB_HEADS * HEAD_DIM, dq_ref, qn_ref, scale * LOG2E, feature_major=True)
    roped(OFF_DK, GLB_KV_HEADS * HEAD_DIM, dk_ref, kn_ref, 1.0)
    dv_ref[...] = proj(OFF_DV, GLB_KV_HEADS * HEAD_DIM).astype(BF16).T

    for i in range(N_BRANCH):
        gl = proj(OFF_GATES + i * D_MODEL, D_MODEL)
        gt_ref[:, i * D_MODEL:(i + 1) * D_MODEL] = _sigmoid(gl).astype(BF16)


def _proj_call(dims, xall, mod, g, w_in, cos_t, sa_t, sb_t, qn2, kn2, tm):
    B, S, Lc, R = dims
    nlat = B * S // tm
    n_rope_lat = S // tm

    def row(i):
        return (i, 0)

    def mod_idx(i):
        return (jnp.minimum(i * tm // S, B), 0, 0)

    def rope_idx(i):
        return (jnp.where(i < nlat, i % n_rope_lat, n_rope_lat), 0)

    outs = [(CONV_DIM, F32, False), (512, BF16, True), (128, BF16, False), (128, BF16, True),
            (256, BF16, False), (256, BF16, False), (512, BF16, False), (512, F32, False),
            (512, BF16, True), (128, BF16, False), (128, BF16, True), (4 * D_MODEL, BF16, False)]
    return pl.pallas_call(
        _proj_kernel,
        grid=(R // tm,),
        in_specs=[
            pl.BlockSpec((tm, D_MODEL), row),
            pl.BlockSpec((1, 1, 6 * D_MODEL), mod_idx),
            pl.BlockSpec((1, D_MODEL), lambda i: (0, 0)),
            _resident((D_MODEL, IN_DIM)),
            pl.BlockSpec((tm, LANES), rope_idx),
            pl.BlockSpec((tm, LANES), rope_idx),
            pl.BlockSpec((tm, LANES), rope_idx),
            pl.BlockSpec((1, LANES), lambda i: (0, 0)),
            pl.BlockSpec((1, LANES), lambda i: (0, 0)),
        ],
        out_specs=[pl.BlockSpec((w, tm), lambda i: (0, i)) if fm else pl.BlockSpec((tm, w), row)
                   for w, _, fm in outs],
        out_shape=[jax.ShapeDtypeStruct((w, R) if fm else (R, w), dt) for w, dt, fm in outs],
        compiler_params=_cparams("arbitrary"),
        name="proj",
    )(xall, mod, g, w_in, cos_t, sa_t, sb_t, qn2, kn2)


CONV_HALO = 16
CONV_ROWS = 32


def _seq_edges(i, tile, n_lat_rows, S, Lc):
    r0 = i * tile
    in_lat = r0 < n_lat_rows
    pos = jnp.where(in_lat, r0 % S, (r0 - n_lat_rows) % Lc)
    length = jnp.where(in_lat, S, Lc)
    return pos == 0, pos + tile == length


def _conv_kernel(dims, tc, z_ref, zp_ref, zn_ref, w_ref, b_ref, lg_ref, lb_ref, o_ref, buf):
    B, S, Lc, R = dims
    first, last = _seq_edges(pl.program_id(0), tc, B * S, S, Lc)
    rows = tc + 2 * CONV_HALO
    zpad = jnp.concatenate([jnp.where(first, 0.0, zp_ref[...]), z_ref[...],
                            jnp.where(last, 0.0, zn_ref[...])], axis=0)
    buf[0] = zpad
    for r in range(1, SUBLANES):
        buf[r] = pltpu.roll(zpad, rows - r, 0)
    pad = (CONV_WIDTH - 1) // 2
    bias, lg, lb = b_ref[...], lg_ref[...], lb_ref[...]

    for c in range(tc // CONV_ROWS):
        r0 = c * CONV_ROWS
        acc = jnp.zeros((CONV_ROWS, CONV_DIM), F32) + bias
        for k in range(CONV_WIDTH):
            off = CONV_HALO - pad + k
            base = r0 + (off // SUBLANES) * SUBLANES
            acc = acc + w_ref[k:k + 1, :] * buf[off % SUBLANES, base:base + CONV_ROWS, :]
        mu = jnp.mean(acc, axis=-1, keepdims=True)
        xc = acc - mu
        var = jnp.mean(xc * xc, axis=-1, keepdims=True)
        y = xc * lax.rsqrt(var + EPS) * lg + lb
        o_ref[r0:r0 + CONV_ROWS, :] = _silu(y).astype(o_ref.dtype)


def _conv_call(dims, z, a_dw, a_dw_b, a_ln_g, a_ln_b, tc):
    B, S, Lc, R = dims
    assert S % tc == 0 and Lc % tc == 0
    hb = tc // CONV_HALO
    nhb = R // CONV_HALO
    return pl.pallas_call(
        functools.partial(_conv_kernel, dims, tc),
        grid=(R // tc,),
        in_specs=[
            pl.BlockSpec((tc, CONV_DIM), lambda i: (i, 0)),
            pl.BlockSpec((CONV_HALO, CONV_DIM), lambda i: (jnp.maximum(i * hb - 1, 0), 0)),
            pl.BlockSpec((CONV_HALO, CONV_DIM), lambda i: (jnp.minimum((i + 1) * hb, nhb - 1), 0)),
            pl.BlockSpec((CONV_WIDTH, CONV_DIM), lambda i: (0, 0)),
            pl.BlockSpec((1, CONV_DIM), lambda i: (0, 0)),
            pl.BlockSpec((1, CONV_DIM), lambda i: (0, 0)),
            pl.BlockSpec((1, CONV_DIM), lambda i: (0, 0)),
        ],
        out_specs=pl.BlockSpec((tc, CONV_DIM), lambda i: (i, 0)),
        out_shape=jax.ShapeDtypeStruct((R, CONV_DIM), BF16),
        scratch_shapes=[pltpu.VMEM((SUBLANES, tc + 2 * CONV_HALO, CONV_DIM), F32)],
        compiler_params=_cparams("arbitrary"),
        name="conv_branch",
    )(z, z, z, a_dw, a_dw_b.reshape(1, -1), a_ln_g.reshape(1, -1), a_ln_b.reshape(1, -1))


def _stacked_q_weights(qT, kv_heads, grp):
    zeros = jnp.zeros((HEAD_DIM, grp * BLOCK), BF16)
    out = []
    for kvh in range(kv_heads):
        piece = jnp.concatenate([qT[(kvh * grp + g) * HEAD_DIM:(kvh * grp + g + 1) * HEAD_DIM, :]
                                 for g in range(grp)], axis=1)
        out.append(jnp.concatenate([piece, zeros] if kvh == 0 else [zeros, piece], axis=0))
    return out


def _unstack_heads(oT, kv_heads, grp):
    OT = jnp.concatenate([oT[kvh][:, g * BLOCK:(g + 1) * BLOCK]
                          for kvh in range(kv_heads) for g in range(grp)], axis=0)
    return OT.T


WIN_PAIR = 2


def _window_kernel(dims, qT_ref, kp_ref, km_ref, kn_ref, kx_ref, vp_ref, vm_ref, vn_ref, vx_ref,
                   sink_ref, o_ref):
    B, S, Lc, R = dims
    nqb = S // BLOCK
    jj = pl.program_id(1)
    is_lat = jj < nqb // WIN_PAIR
    grp = WIN_HEADS // WIN_KV_HEADS
    nk = 3 * BLOCK + Lc

    key = lax.broadcasted_iota(jnp.int32, (nk, BLOCK), 0)
    qry = lax.broadcasted_iota(jnp.int32, (nk, BLOCK), 1)
    d = key - qry
    in_band = (d >= BLOCK - WINDOW) & (d <= BLOCK + WINDOW)
    km, vm, kx, vx = km_ref[...], vm_ref[...], kx_ref[...], vx_ref[...]
    subs = []
    for s in range(WIN_PAIR):
        j = jj * WIN_PAIR + s
        lo = jnp.where(j >= 1, 0, BLOCK)
        hi = jnp.where(is_lat, jnp.where(j <= nqb - 2, 3 * BLOCK, 2 * BLOCK), 0)
        bias = jnp.where((in_band & (key >= lo) & (key < hi)) | (key >= 3 * BLOCK), 0.0, NEG_INF)
        bias = jnp.concatenate([bias] * grp, axis=1)
        if s == 0:
            kk = jnp.concatenate([kp_ref[...], km, kx], axis=0)
            vT = jnp.concatenate([vp_ref[...], vm, vx], axis=1)
        else:
            kk = jnp.concatenate([km, kn_ref[...], kx], axis=0)
            vT = jnp.concatenate([vm, vn_ref[...], vx], axis=1)
        W = _stacked_q_weights(qT_ref[:, s * BLOCK:(s + 1) * BLOCK], WIN_KV_HEADS, grp)
        subs.append((bias, kk, vT, W))

    scores = [[jnp.dot(kk, W[kvh], preferred_element_type=F32) for kvh in range(WIN_KV_HEADS)]
              for (_, kk, _, W) in subs]
    probs = []
    for s, (bias, _, _, _) in enumerate(subs):
        row = []
        for kvh in range(WIN_KV_HEADS):
            sk = jnp.concatenate([jnp.full((1, BLOCK), sink_ref[kvh * grp + g] * LOG2E, F32)
                                  for g in range(grp)], axis=1)
            sT = scores[s][kvh] + bias
            m = jnp.maximum(jnp.max(sT, axis=0, keepdims=True), sk)
            pT = jnp.exp2(sT - m)
            l = jnp.sum(pT, axis=0, keepdims=True) + jnp.exp2(sk - m)
            row.append((pT.astype(BF16), 1.0 / l))
        probs.append(row)
    for s, (_, _, vT, _) in enumerate(subs):
        oT = [jnp.dot(vT[kvh * HEAD_DIM:(kvh + 1) * HEAD_DIM, :], probs[s][kvh][0],
                      preferred_element_type=F32) * probs[s][kvh][1] for kvh in range(WIN_KV_HEADS)]
        o_ref[s * BLOCK:(s + 1) * BLOCK, :] = _unstack_heads(oT, WIN_KV_HEADS, grp).astype(o_ref.dtype)


def _window_call(dims, bqT, bk, bvT, sink, ctx_queries):
    B, S, Lc, R = dims
    nqb, ncb = S // BLOCK, Lc // BLOCK
    assert nqb % WIN_PAIR == 0 and ncb % WIN_PAIR == 0
    npl, npc = nqb // WIN_PAIR, ncb // WIN_PAIR
    n_steps = npl + (npc if ctx_queries else 0)
    kvw = WIN_KV_HEADS * HEAD_DIM
    qw = WIN_HEADS * HEAD_DIM
    pair_rows = WIN_PAIR * BLOCK

    def q_blk(b, j):
        return jnp.where(j < npl, b * npl + j, B * npl + b * npc + (j - npl))

    def mid(b, j):
        return b * npl + jnp.minimum(j, npl - 1)

    def prev(b, j):
        return b * nqb + jnp.clip(j * WIN_PAIR - 1, 0, nqb - 1)

    def nxt(b, j):
        return b * nqb + jnp.clip(j * WIN_PAIR + WIN_PAIR, 0, nqb - 1)

    kspec = [pl.BlockSpec((BLOCK, kvw), lambda b, j: (prev(b, j), 0)),
             pl.BlockSpec((pair_rows, kvw), lambda b, j: (mid(b, j), 0)),
             pl.BlockSpec((BLOCK, kvw), lambda b, j: (nxt(b, j), 0)),
             pl.BlockSpec((Lc, kvw), lambda b, j: (B * S // Lc + b, 0))]
    vspec = [pl.BlockSpec((kvw, BLOCK), lambda b, j: (0, prev(b, j))),
             pl.BlockSpec((kvw, pair_rows), lambda b, j: (0, mid(b, j))),
             pl.BlockSpec((kvw, BLOCK), lambda b, j: (0, nxt(b, j))),
             pl.BlockSpec((kvw, Lc), lambda b, j: (0, B * S // Lc + b))]
    return pl.pallas_call(
        functools.partial(_window_kernel, dims),
        grid=(B, n_steps),
        in_specs=[pl.BlockSpec((qw, pair_rows), lambda b, j: (0, q_blk(b, j)))] + kspec + vspec
        + [pl.BlockSpec(memory_space=pltpu.SMEM)],
        out_specs=pl.BlockSpec((pair_rows, qw), lambda b, j: (q_blk(b, j), 0)),
        out_shape=jax.ShapeDtypeStruct((R, qw), BF16),
        compiler_params=_cparams("arbitrary", "arbitrary"),
        name="window_attn",
    )(bqT, bk, bk, bk, bk, bvT, bvT, bvT, bvT, sink)


GLB_TK = 128
GLB_ONLINE_TK = 256
GLB_MAX_SAFE_BOUND = 40.0


def _global_kernel(dims, par_ref, qT_ref, kl_ref, kx_ref, vTl_ref, vTx_ref, o_ref):
    B, S, Lc, R = dims
    nqb = S // BLOCK
    is_lat = pl.program_id(1) < nqb
    grp = GLB_HEADS // GLB_KV_HEADS
    cols = grp * BLOCK
    shift = par_ref[0]
    W = _stacked_q_weights(qT_ref[...], GLB_KV_HEADS, grp)

    def scores(kk):
        return tuple(jnp.dot(kk, W[kvh], preferred_element_type=F32) for kvh in range(GLB_KV_HEADS))

    def v_rows(vT, kvh):
        return vT[kvh * HEAD_DIM:(kvh + 1) * HEAD_DIM, :]

    def bounded(chunks):
        ls = [jnp.zeros((1, cols), F32)] * GLB_KV_HEADS
        accs = [jnp.zeros((HEAD_DIM, cols), F32)] * GLB_KV_HEADS
        s_cur = scores(chunks[0][0][chunks[0][2]:chunks[0][2] + GLB_TK, :])
        for i, (k_ref, v_ref, r0) in enumerate(chunks):
            s_next = None
            if i + 1 < len(chunks):
                nk, _, nr = chunks[i + 1]
                s_next = scores(nk[nr:nr + GLB_TK, :])
            vT = v_ref[:, r0:r0 + GLB_TK]
            for kvh in range(GLB_KV_HEADS):
                pT = jnp.exp2(s_cur[kvh] - shift)
                ls[kvh] = ls[kvh] + jnp.sum(pT, axis=0, keepdims=True)
                accs[kvh] = accs[kvh] + jnp.dot(v_rows(vT, kvh), pT.astype(BF16), preferred_element_type=F32)
            s_cur = s_next
        return tuple(acc * (1.0 / l) for l, acc in zip(ls, accs))

    def online():
        def update(carry, kk, vT):
            new = []
            for kvh, sT in enumerate(scores(kk)):
                m, l, acc = carry[kvh]
                m_new = jnp.maximum(m, jnp.max(sT, axis=0, keepdims=True))
                alpha = jnp.exp2(m - m_new)
                pT = jnp.exp2(sT - m_new)
                l = alpha * l + jnp.sum(pT, axis=0, keepdims=True)
                acc = alpha * acc + jnp.dot(v_rows(vT, kvh), pT.astype(BF16), preferred_element_type=F32)
                new.append((m_new, l, acc))
            return tuple(new)

        init = tuple((jnp.full((1, cols), NEG_INF, F32), jnp.zeros((1, cols), F32),
                      jnp.zeros((HEAD_DIM, cols), F32)) for _ in range(GLB_KV_HEADS))
        carry = init
        for r0 in range(0, Lc, GLB_ONLINE_TK):
            carry = update(carry, kx_ref[r0:r0 + GLB_ONLINE_TK, :], vTx_ref[:, r0:r0 + GLB_ONLINE_TK])

        def body(c, carry):
            r0 = pl.multiple_of(c * GLB_ONLINE_TK, GLB_ONLINE_TK)
            return update(carry, kl_ref[pl.ds(r0, GLB_ONLINE_TK), :], vTl_ref[:, pl.ds(r0, GLB_ONLINE_TK)])

        carry = lax.fori_loop(0, jnp.where(is_lat, S // GLB_ONLINE_TK, 0), body, carry)
        return tuple(acc * (1.0 / l) for (m, l, acc) in carry)

    ctx_chunks = [(kx_ref, vTx_ref, r0) for r0 in range(0, Lc, GLB_TK)]
    lat_chunks = [(kl_ref, vTl_ref, r0) for r0 in range(0, S, GLB_TK)]

    def bounded_any():
        return lax.cond(is_lat, lambda: bounded(ctx_chunks + lat_chunks), lambda: bounded(ctx_chunks))

    oT = lax.cond(par_ref[1] > 0.5, bounded_any, online)
    o_ref[...] = _unstack_heads(oT, GLB_KV_HEADS, grp).astype(o_ref.dtype)


def _global_call(dims, par, dqT, dk, dvT, ctx_queries):
    B, S, Lc, R = dims
    nqb, ncb = S // BLOCK, Lc // BLOCK
    n_steps = nqb + (ncb if ctx_queries else 0)
    lat_blocks = B * nqb
    kvw = GLB_KV_HEADS * HEAD_DIM
    qw = GLB_HEADS * HEAD_DIM
    assert Lc % GLB_ONLINE_TK == 0 and S % GLB_ONLINE_TK == 0 and GLB_ONLINE_TK % GLB_TK == 0

    def q_blk(b, j):
        return jnp.where(j < nqb, b * nqb + j, lat_blocks + b * ncb + (j - nqb))

    return pl.pallas_call(
        functools.partial(_global_kernel, dims),
        grid=(B, n_steps),
        in_specs=[pl.BlockSpec(memory_space=pltpu.SMEM),
                  pl.BlockSpec((qw, BLOCK), lambda b, j: (0, q_blk(b, j))),
                  pl.BlockSpec((S, kvw), lambda b, j: (b, 0)),
                  pl.BlockSpec((Lc, kvw), lambda b, j: (B * S // Lc + b, 0)),
                  pl.BlockSpec((kvw, S), lambda b, j: (0, b)),
                  pl.BlockSpec((kvw, Lc), lambda b, j: (0, B * S // Lc + b))],
        out_specs=pl.BlockSpec((BLOCK, qw), lambda b, j: (q_blk(b, j), 0)),
        out_shape=jax.ShapeDtypeStruct((R, qw), BF16),
        compiler_params=_cparams("arbitrary", "arbitrary"),
        name="global_attn",
    )(par, dqT, dk, dk, dvT, dvT)


RET_HPS = 2
RET_KERNEL_CHUNK = 256


def _ret_kernel(dims, ql_ref, qx_ref, kl_ref, kx_ref, vl_ref, vx_ref, gl_ref, gx_ref, lg_ref, gn_ref,
                ol_ref, ox_ref, fl_buf, fx_buf):
    B, S, Lc, R = dims
    C = RET_KERNEL_CHUNK
    assert S % C == 0 and Lc % C == 0
    hp = pl.program_id(1)
    pos_r = lax.broadcasted_iota(jnp.int32, (C, C), 0).astype(F32)
    pos_c = lax.broadcasted_iota(jnp.int32, (C, C), 1).astype(F32)
    diff = pos_r - pos_c
    col_pos = lax.broadcasted_iota(jnp.int32, (C, 1), 0).astype(F32)

    tabs = {}
    for dr in range(2):
        for hh in range(RET_HPS):
            lg = lg_ref[dr, hp * RET_HPS + hh]
            dd = diff if dr == 0 else -diff
            dmat = jnp.where(dd >= 0, jnp.exp(lg * jnp.maximum(dd, 0.0)), 0.0)
            p_eff = col_pos if dr == 0 else (C - 1.0) - col_pos
            kdec = jnp.exp(lg * ((C - 1.0) - p_eff))
            qdec = jnp.exp(lg * (p_eff + 1.0))
            cdec = jnp.exp(jnp.zeros((1, RET_V_DIM), F32) + lg * float(C))
            tabs[dr, hh] = (dmat, kdec, qdec, cdec)

    qk_w = RET_HPS * RET_QK_DIM
    lane = lax.broadcasted_iota(jnp.int32, (1, qk_w), 1)
    row = lax.broadcasted_iota(jnp.int32, (qk_w, 1), 0)
    lane_m = [jnp.where((lane >= hh * RET_QK_DIM) & (lane < (hh + 1) * RET_QK_DIM), 1.0, 0.0)
              for hh in range(RET_HPS)]
    row_m = [jnp.where((row >= hh * RET_QK_DIM) & (row < (hh + 1) * RET_QK_DIM), 1.0, 0.0)
             for hh in range(RET_HPS)]

    def step(q_ref, k_ref, v_ref, r0, dr, hh, state):
        dmat, kdec, qdec, cdec = tabs[dr, hh]
        vs = slice(hh * RET_V_DIM, (hh + 1) * RET_V_DIM)
        qf = q_ref[pl.ds(r0, C), :].astype(F32) * lane_m[hh]
        k2 = k_ref[pl.ds(r0, C), :]
        vc = v_ref[pl.ds(r0, C), vs]
        sc = lax.dot_general(qf.astype(BF16), k2, (((1,), (1,)), ((), ())), preferred_element_type=F32) * dmat
        inner = jnp.dot(sc.astype(BF16), vc, preferred_element_type=F32)
        qd = (qf * qdec).astype(BF16)
        cross = jnp.dot(qd, state.astype(BF16), preferred_element_type=F32)
        kd = (k2.astype(F32) * kdec).astype(BF16)
        u = lax.dot_general(kd, vc, (((0,), (0,)), ((), ())), preferred_element_type=F32)
        return inner + cross, cdec * state + u * row_m[hh]

    def sweep(q_ref, k_ref, v_ref, f_buf, n_chunks, states):
        def body(i, st):
            st = list(st)
            rf = pl.multiple_of(i * C, C)
            rb = pl.multiple_of((n_chunks - 1 - i) * C, C)
            for hh in range(RET_HPS):
                vs = slice(hh * RET_V_DIM, (hh + 1) * RET_V_DIM)
                of, st[hh] = step(q_ref, k_ref, v_ref, rf, 0, hh, st[hh])
                f_buf[0, pl.ds(rf, C), vs] = of
                ob, st[RET_HPS + hh] = step(q_ref, k_ref, v_ref, rb, 1, hh, st[RET_HPS + hh])
                f_buf[1, pl.ds(rb, C), vs] = ob
            return tuple(st)
        return lax.fori_loop(0, n_chunks, body, states)

    def finish(f_buf, g_ref, o_ref, n_chunks):
        def body(i, carry):
            r0 = pl.multiple_of(i * C, C)
            for hh in range(RET_HPS):
                vs = slice(hh * RET_V_DIM, (hh + 1) * RET_V_DIM)
                o = f_buf[0, pl.ds(r0, C), vs] + f_buf[1, pl.ds(r0, C), vs]
                mu = jnp.mean(o, axis=-1, keepdims=True)
                xc = o - mu
                var = jnp.mean(xc * xc, axis=-1, keepdims=True)
                y = xc * lax.rsqrt(var + EPS) * gn_ref[:, vs]
                o_ref[pl.ds(r0, C), vs] = (_silu(g_ref[pl.ds(r0, C), vs]) * y).astype(o_ref.dtype)
            return carry
        lax.fori_loop(0, n_chunks, body, 0)

    zero = jnp.zeros((qk_w, RET_V_DIM), F32)
    states = sweep(qx_ref, kx_ref, vx_ref, fx_buf, Lc // C, (zero,) * (2 * RET_HPS))
    sweep(ql_ref, kl_ref, vl_ref, fl_buf, S // C, states)
    finish(fx_buf, gx_ref, ox_ref, Lc // C)
    finish(fl_buf, gl_ref, ol_ref, S // C)


def _ret_call(dims, cq, ck, cv, cg, log_gamma, gn_g):
    B, S, Lc, R = dims
    qw = RET_HPS * RET_QK_DIM
    vw = RET_HPS * RET_V_DIM
    nhp = RET_HEADS // RET_HPS
    xoff = B * S // Lc

    def lat(w):
        return pl.BlockSpec((S, w), lambda b, h: (b, h))

    def ctx(w):
        return pl.BlockSpec((Lc, w), lambda b, h: (xoff + b, h))

    return pl.pallas_call(
        functools.partial(_ret_kernel, dims),
        grid=(B, nhp),
        in_specs=[lat(qw), ctx(qw), lat(qw), ctx(qw), lat(vw), ctx(vw), lat(vw), ctx(vw),
                  pl.BlockSpec(memory_space=pltpu.SMEM),
                  pl.BlockSpec((1, vw), lambda b, h: (0, h))],
        out_specs=[pl.BlockSpec((S, vw), lambda b, h: (b, h)),
                   pl.BlockSpec((Lc, vw), lambda b, h: (b, h))],
        out_shape=[jax.ShapeDtypeStruct((B * S, RET_HEADS * RET_V_DIM), BF16),
                   jax.ShapeDtypeStruct((B * Lc, RET_HEADS * RET_V_DIM), BF16)],
        scratch_shapes=[pltpu.VMEM((2, S, vw), F32), pltpu.VMEM((2, Lc, vw), F32)],
        compiler_params=_cparams("arbitrary", "arbitrary"),
        name="retention",
    )(cq, cq, ck, ck, cv, cv, cg, cg, log_gamma, gn_g.reshape(1, -1))


MERGE_SUB = 256


def _merge_kernel(n_lat_tiles, ya_ref, yb_ref, yrl_ref, yrx_ref, yd_ref, gt_ref, x_ref, mod_ref, g_ref,
                  wbr_ref, wo_ref, o_ref):
    tm = x_ref.shape[0]
    is_lat = pl.program_id(0) < n_lat_tiles
    gate = mod_ref[0][:, 2 * D_MODEL:3 * D_MODEL]
    subs = [slice(r0, r0 + MERGE_SUB) for r0 in range(0, tm, MERGE_SUB)]

    def branch_dots(rs):
        yr = jnp.where(is_lat, yrl_ref[rs, :], yrx_ref[rs, :])
        return [jnp.dot(b, wbr_ref[i], preferred_element_type=F32)
                for i, b in enumerate((ya_ref[rs, :], yb_ref[rs, :], yr, yd_ref[rs, :]))]

    dots = [branch_dots(rs) for rs in subs]
    for rs, d in zip(subs, dots):
        acc = None
        for i in range(N_BRANCH):
            t = gt_ref[rs, i * D_MODEL:(i + 1) * D_MODEL].astype(F32) * d[i]
            acc = t if acc is None else acc + t
        y = jnp.dot(acc.astype(BF16), wo_ref[...], preferred_element_type=F32)
        r = y * lax.rsqrt(jnp.mean(y * y, axis=-1, keepdims=True) + EPS) * g_ref[...]
        o_ref[rs, :] = x_ref[rs, :] + gate * r


def _merge_call(dims, ya, yb, yr_l, yr_x, yd, gates, xall, mod, g, w_br, w_o, tm, rows_out):
    B, S, Lc, R = dims
    assert tm % MERGE_SUB == 0
    n_lat = B * S // tm
    n_ctx = B * Lc // tm

    def row(i):
        return (i, 0)

    br = pl.BlockSpec((tm, BRANCH_DIM), row)
    return pl.pallas_call(
        functools.partial(_merge_kernel, n_lat),
        grid=(rows_out // tm,),
        in_specs=[br, br,
                  pl.BlockSpec((tm, BRANCH_DIM), lambda i: (jnp.minimum(i, n_lat - 1), 0)),
                  pl.BlockSpec((tm, BRANCH_DIM), lambda i: (jnp.clip(i - n_lat, 0, n_ctx - 1), 0)),
                  br,
                  pl.BlockSpec((tm, N_BRANCH * D_MODEL), row),
                  pl.BlockSpec((tm, D_MODEL), row),
                  pl.BlockSpec((1, 1, 6 * D_MODEL), lambda i: (jnp.minimum(i * tm // S, B), 0, 0)),
                  pl.BlockSpec((1, D_MODEL), lambda i: (0, 0)),
                  _resident((N_BRANCH, BRANCH_DIM, D_MODEL)),
                  _resident((D_MODEL, D_MODEL))],
        out_specs=pl.BlockSpec((tm, D_MODEL), row),
        out_shape=jax.ShapeDtypeStruct((rows_out, D_MODEL), F32),
        compiler_params=_cparams("arbitrary"),
        name="merge",
    )(ya, yb, yr_l, yr_x, yd, gates, xall, mod, g, w_br, w_o)


FFN_HALO = 8
FFN_CN = 256
FFN_LOOKAHEAD = 3


def _ffn_kernel(dims, tm, x_ref, xp_ref, xn_ref, mod_ref, g2_ref, g3_ref, wup_ref, dw_ref, db_ref, wdn_ref,
                o_ref, acc_buf):
    B, S, Lc, R = dims
    first, last = _seq_edges(pl.program_id(0), tm, B * S, S, Lc)
    m = mod_ref[0]
    g2 = g2_ref[...]
    x = x_ref[...]
    hp = jnp.where(first, 0.0, _modulated(xp_ref[...], g2, m, 1))
    hn = jnp.where(last, 0.0, _modulated(xn_ref[...], g2, m, 1))
    h = jnp.concatenate([hp, _modulated(x, g2, m, 1), hn], axis=0).astype(BF16)

    rows = tm + 2 * FFN_HALO

    def up(j):
        return tuple(jnp.dot(h, wup_ref[:, c0:c0 + FFN_CN], preferred_element_type=F32)
                     for c0 in (j * FFN_CN, D_FF + j * FFN_CN))

    def conv3(u, col0):
        w = dw_ref[:, col0:col0 + FFN_CN]
        y = (w[0:1] * pltpu.roll(u, 1, 0) + w[1:2] * u + w[2:3] * pltpu.roll(u, rows - 1, 0)
             + db_ref[:, col0:col0 + FFN_CN])
        return y[FFN_HALO:FFN_HALO + tm]

    n_chunks = D_FF // FFN_CN
    ahead = [up(j) for j in range(FFN_LOOKAHEAD)]
    for j in range(n_chunks):
        ca, cb = j * FFN_CN, D_FF + j * FFN_CN
        if j + FFN_LOOKAHEAD < n_chunks:
            ahead.append(up(j + FFN_LOOKAHEAD))
        u_cur = ahead.pop(0)
        a, b = conv3(u_cur[0], ca), conv3(u_cur[1], cb)
        t = (_silu(a) * b).astype(BF16)
        part = jnp.dot(t, wdn_ref[ca:ca + FFN_CN, :], preferred_element_type=F32)
        if j == 0:
            acc_buf[...] = part
        else:
            acc_buf[...] += part

    y = acc_buf[...]
    r = y * lax.rsqrt(jnp.mean(y * y, axis=-1, keepdims=True) + EPS) * g3_ref[...]
    o_ref[...] = x + m[:, 5 * D_MODEL:6 * D_MODEL] * r


def _ffn_call(dims, xall, mod, g2, g3, f_up, f_dw, f_dw_b, f_down, tm):
    B, S, Lc, R = dims
    rows = xall.shape[0]
    assert rows in (R, B * S) and S % tm == 0 and Lc % tm == 0
    hb = tm // FFN_HALO
    nhb = rows // FFN_HALO
    return pl.pallas_call(
        functools.partial(_ffn_kernel, dims, tm),
        grid=(rows // tm,),
        in_specs=[
            pl.BlockSpec((tm, D_MODEL), lambda i: (i, 0)),
            pl.BlockSpec((FFN_HALO, D_MODEL), lambda i: (jnp.maximum(i * hb - 1, 0), 0)),
            pl.BlockSpec((FFN_HALO, D_MODEL), lambda i: (jnp.minimum((i + 1) * hb, nhb - 1), 0)),
            pl.BlockSpec((1, 1, 6 * D_MODEL), lambda i: (jnp.minimum(i * tm // S, B), 0, 0)),
            pl.BlockSpec((1, D_MODEL), lambda i: (0, 0)),
            pl.BlockSpec((1, D_MODEL), lambda i: (0, 0)),
            _resident((D_MODEL, 2 * D_FF)),
            pl.BlockSpec((FFN_CONV_WIDTH, 2 * D_FF), lambda i: (0, 0)),
            pl.BlockSpec((1, 2 * D_FF), lambda i: (0, 0)),
            _resident((D_FF, D_MODEL)),
        ],
        out_specs=pl.BlockSpec((tm, D_MODEL), lambda i: (i, 0)),
        out_shape=jax.ShapeDtypeStruct((rows, D_MODEL), F32),
        scratch_shapes=[pltpu.VMEM((tm, D_MODEL), F32)],
        compiler_params=_cparams("arbitrary"),
        name="conv_ffn",
    )(xall, xall, xall, mod, g2, g3, f_up, f_dw, f_dw_b.reshape(1, -1), f_down)


def _rope_tables(S, pad_rows):
    rows = S // GRID_W
    row = jnp.repeat(jnp.arange(rows), GRID_W).astype(F32)
    col = jnp.tile(jnp.arange(GRID_W), rows).astype(F32)
    half = HEAD_DIM // 2
    inv = ROPE_BASE ** (-jnp.arange(0, half, 2, dtype=F32) / half)
    ang = jnp.concatenate([row[:, None] * inv, col[:, None] * inv], axis=-1)
    cos, sin = jnp.cos(ang), jnp.sin(ang)
    zero = jnp.zeros_like(sin)
    cos_h = jnp.concatenate([cos, cos], axis=-1)
    sa_h = jnp.concatenate([-sin, zero], axis=-1)
    sb_h = jnp.concatenate([zero, sin], axis=-1)
    reps = LANES // HEAD_DIM

    def slab(t, fill):
        t = jnp.tile(t, (1, reps))
        return jnp.concatenate([t, jnp.full((pad_rows, LANES), fill, F32)], axis=0)

    return slab(cos_h, 1.0), slab(sa_h, 0.0), slab(sb_h, 0.0)


def kernel(x, c, ctx, c_ctx, ada_w, ada_b, norm_g, w_in, a_dw, a_dw_b, a_ln_g, a_ln_b, b_sink,
           c_decay_logit, c_gn_g, d_qn_g, d_kn_g, w_br, w_o, f_up, f_dw, f_dw_b, f_down):
    B, S, _ = x.shape
    Lc = ctx.shape[1]
    R = B * (S + Lc)
    dims = (B, S, Lc, R)
    tm = ROW_TILE
    assert B + 1 <= MOD_ROWS and S % tm == 0 and (B * Lc) % tm == 0 and Lc % BLOCK == 0 and S % Lc == 0

    cvec = jnp.concatenate([c, c_ctx[None, :], jnp.zeros((MOD_ROWS - B - 1, D_MODEL), F32)], axis=0)
    mods = _ada_call(cvec, ada_w, ada_b).reshape(DEPTH, MOD_ROWS, 1, 6 * D_MODEL)
    cos_t, sa_t, sb_t = _rope_tables(S, tm)
    reps = LANES // HEAD_DIM
    log_gamma = jnp.log(jax.nn.sigmoid(c_decay_logit.astype(F32)))

    xall = jnp.concatenate([x.reshape(B * S, D_MODEL), ctx.reshape(B * Lc, D_MODEL)], axis=0)
    for l in range(DEPTH):
        mod = mods[l]
        ng = norm_g[l].reshape(4, 1, D_MODEL)
        (z, bq, bk, bv, cq, ck, cv, cg, dq, dk, dv, gates) = _proj_call(
            dims, xall, mod, ng[0], w_in[l].astype(BF16), cos_t, sa_t, sb_t,
            jnp.tile(d_qn_g[l], reps)[None, :], jnp.tile(d_kn_g[l], reps)[None, :], tm)
        with_ctx = l < DEPTH - 1
        ya = _conv_call(dims, z, a_dw[l], a_dw_b[l], a_ln_g[l], a_ln_b[l], 256)
        yb = _window_call(dims, bq, bk, bv, b_sink[l], with_ctx)
        yr_l, yr_x = _ret_call(dims, cq, ck, cv, cg, log_gamma[l], c_gn_g[l])
        bound = 1.02 * HEAD_DIM * (HEAD_DIM ** -0.5) * jnp.max(jnp.abs(d_qn_g[l])) * jnp.max(jnp.abs(d_kn_g[l]))
        par = jnp.stack([bound * LOG2E, (bound <= GLB_MAX_SAFE_BOUND).astype(F32)])
        yd = _global_call(dims, par, dq, dk, dv, with_ctx)
        rows_out = R if with_ctx else B * S
        xall = _merge_call(dims, ya, yb, yr_l, yr_x, yd, gates, xall, mod, ng[1],
                           w_br[l].astype(BF16), w_o[l].astype(BF16), 2 * MERGE_SUB, rows_out)
        xall = _ffn_call(dims, xall, mod, ng[2], ng[3], f_up[l].astype(BF16), f_dw[l], f_dw_b[l],
                         f_down[l].astype(BF16), tm)
    return xall.reshape(B, S, D_MODEL)
```

```python
import functools

import jax
import jax.numpy as jnp
import numpy as np
from jax import lax
from jax.experimental import pallas as pl
from jax.experimental.pallas import tpu as pltpu

D_MODEL = 1024
DEPTH = 4
GRID_W = 64
HEAD_DIM = 64
BLOCK = 128
ROPE_BASE = 10000.0
EPS = 1e-6
NEG_INF = -1e30
CONV_DIM = 512
CONV_WIDTH = 31
WIN_HEADS = 8
WIN_KV_HEADS = 2
WINDOW = 128
RET_HEADS = 4
RET_QK_DIM = 64
RET_V_DIM = 128
RET_CHUNK = 128
GLB_HEADS = 8
GLB_KV_HEADS = 2
N_BRANCH = 4
BRANCH_DIM = 512
D_FF = 2816
FFN_CONV_WIDTH = 3

IN_SIZES = (2 * CONV_DIM,
            WIN_HEADS * HEAD_DIM, WIN_KV_HEADS * HEAD_DIM, WIN_KV_HEADS * HEAD_DIM,
            RET_HEADS * RET_QK_DIM, RET_HEADS * RET_QK_DIM, RET_HEADS * RET_V_DIM, RET_HEADS * RET_V_DIM,
            GLB_HEADS * HEAD_DIM, GLB_KV_HEADS * HEAD_DIM, GLB_KV_HEADS * HEAD_DIM,
            N_BRANCH * D_MODEL)
IN_DIM = sum(IN_SIZES)
IN_OFF = tuple(int(v) for v in np.cumsum((0,) + IN_SIZES)[:-1])
(OFF_AU, OFF_BQ, OFF_BK, OFF_BV, OFF_CQ, OFF_CK, OFF_CV, OFF_CG,
 OFF_DQ, OFF_DK, OFF_DV, OFF_GATES) = IN_OFF

LANES = 128
SUBLANES = 8
MOD_ROWS = 16
ROW_TILE = 256
VMEM_LIMIT = 56 * 1024 * 1024
F32 = jnp.float32
BF16 = jnp.bfloat16
LOG2E = 1.4426950408889634


def _sigmoid(v):
    return 1.0 / (1.0 + jnp.exp(-v))


def _silu(v):
    return v * _sigmoid(v)


def _cparams(*sem):
    return pltpu.CompilerParams(dimension_semantics=sem, vmem_limit_bytes=VMEM_LIMIT)


def _resident(shape):
    nd = len(shape)
    return pl.BlockSpec(shape, lambda *_: (0,) * nd, pipeline_mode=pl.Buffered(1))


def _ada_kernel(c_ref, w_ref, b_ref, o_ref):
    a = _silu(c_ref[...]).astype(BF16)
    w = w_ref[0].astype(BF16)
    o_ref[0] = jnp.dot(a, w, preferred_element_type=F32) + b_ref[0]


def _ada_call(cvec, ada_w, ada_b):
    tn = 1536
    n6 = 6 * D_MODEL
    return pl.pallas_call(
        _ada_kernel,
        grid=(DEPTH, n6 // tn),
        in_specs=[
            pl.BlockSpec((MOD_ROWS, D_MODEL), lambda l, j: (0, 0)),
            pl.BlockSpec((1, D_MODEL, tn), lambda l, j: (l, 0, j)),
            pl.BlockSpec((1, 1, tn), lambda l, j: (l, 0, j)),
        ],
        out_specs=pl.BlockSpec((1, MOD_ROWS, tn), lambda l, j: (l, 0, j)),
        out_shape=jax.ShapeDtypeStruct((DEPTH, MOD_ROWS, n6), F32),
        compiler_params=_cparams("arbitrary", "arbitrary"),
        name="ada_mod",
    )(cvec, ada_w, ada_b.reshape(DEPTH, 1, n6))


def _modulated(x, g, m, which):
    o = 3 * which * D_MODEL
    y = x * lax.rsqrt(jnp.mean(x * x, axis=-1, keepdims=True) + EPS) * g
    return y * (1.0 + m[:, o + D_MODEL:o + 2 * D_MODEL]) + m[:, o:o + D_MODEL]


def _rope128(v, cos, sa, sb):
    return v * cos + pltpu.roll(v, LANES - HEAD_DIM // 2, 1) * sa + pltpu.roll(v, HEAD_DIM // 2, 1) * sb


def _head_rms128(v, gain):
    lane = lax.broadcasted_iota(jnp.int32, (1, LANES), 1)
    lo = lane < HEAD_DIM
    sq = v * v
    s0 = jnp.sum(jnp.where(lo, sq, 0.0), axis=-1, keepdims=True)
    s1 = jnp.sum(jnp.where(lo, 0.0, sq), axis=-1, keepdims=True)
    ms = jnp.where(lo, s0, s1) * (1.0 / HEAD_DIM)
    return v * lax.rsqrt(ms + EPS) * gain


def _proj_kernel(x_ref, mod_ref, g_ref, w_ref, cos_ref, sa_ref, sb_ref, qn_ref, kn_ref,
                 z_ref, bq_ref, bk_ref, bv_ref, cq_ref, ck_ref, cv_ref, cg_ref,
                 dq_ref, dk_ref, dv_ref, gt_ref):
    h = _modulated(x_ref[...], g_ref[...], mod_ref[0], 0).astype(BF16)

    def proj(off, width):
        return jnp.dot(h, w_ref[:, off:off + width], preferred_element_type=F32)

    u = proj(OFF_AU, 2 * CONV_DIM)
    z_ref[...] = u[:, :CONV_DIM] * _sigmoid(u[:, CONV_DIM:])

    cos, sa, sb = cos_ref[...], sa_ref[...], sb_ref[...]
    scale = HEAD_DIM ** -0.5

    def roped(off, width, out_ref, gain_ref, mul, feature_major=False):
        v = proj(off, width)
        for c in range(width // LANES):
            s = v[:, c * LANES:(c + 1) * LANES]
            if gain_ref is not None:
                s = _head_rms128(s, gain_ref[...])
            s = _rope128(s, cos, sa, sb) * mul
            if feature_major:
                out_ref[c * LANES:(c + 1) * LANES, :] = s.T.astype(out_ref.dtype)
            else:
                out_ref[:, c * LANES:(c + 1) * LANES] = s.astype(out_ref.dtype)

    roped(OFF_BQ, WIN_HEADS * HEAD_DIM, bq_ref, None, scale * LOG2E, feature_major=True)
    roped(OFF_BK, WIN_KV_HEADS * HEAD_DIM, bk_ref, None, 1.0)
    bv_ref[...] = proj(OFF_BV, WIN_KV_HEADS * HEAD_DIM).astype(BF16).T

    cq_ref[...] = proj(OFF_CQ, RET_HEADS * RET_QK_DIM).astype(BF16)
    ck_ref[...] = (proj(OFF_CK, RET_HEADS * RET_QK_DIM) * (RET_QK_DIM ** -0.5)).astype(BF16)
    cv_ref[...] = proj(OFF_CV, RET_HEADS * RET_V_DIM).astype(BF16)
    cg_ref[...] = proj(OFF_CG, RET_HEADS * RET_V_DIM)

    roped(OFF_DQ, GLB_HEADS * HEAD_DIM, dq_ref, qn_ref, scale * LOG2E, feature_major=True)
    roped(OFF_DK, GLB_KV_HEADS * HEAD_DIM, dk_ref, kn_ref, 1.0)
    dv_ref[...] = proj(OFF_DV, GLB_KV_HEADS * HEAD_DIM).astype(BF16).T

    for i in range(N_BRANCH):
        gl = proj(OFF_GATES + i * D_MODEL, D_MODEL)
        gt_ref[:, i * D_MODEL:(i + 1) * D_MODEL] = _sigmoid(gl).astype(BF16)


def _proj_call(dims, xall, mod, g, w_in, cos_t, sa_t, sb_t, qn2, kn2, tm):
    B, S, Lc, R = dims
    nlat = B * S // tm
    n_rope_lat = S // tm

    def row(i):
        return (i, 0)

    def mod_idx(i):
        return (jnp.minimum(i * tm // S, B), 0, 0)

    def rope_idx(i):
        return (jnp.where(i < nlat, i % n_rope_lat, n_rope_lat), 0)

    outs = [(CONV_DIM, F32, False), (512, BF16, True), (128, BF16, False), (128, BF16, True),
            (256, BF16, False), (256, BF16, False), (512, BF16, False), (512, F32, False),
            (512, BF16, True), (128, BF16, False), (128, BF16, True), (4 * D_MODEL, BF16, False)]
    return pl.pallas_call(
        _proj_kernel,
        grid=(R // tm,),
        in_specs=[
            pl.BlockSpec((tm, D_MODEL), row),
            pl.BlockSpec((1, 1, 6 * D_MODEL), mod_idx),
            pl.BlockSpec((1, D_MODEL), lambda i: (0, 0)),
            _resident((D_MODEL, IN_DIM)),
            pl.BlockSpec((tm, LANES), rope_idx),
            pl.BlockSpec((tm, LANES), rope_idx),
            pl.BlockSpec((tm, LANES), rope_idx),
            pl.BlockSpec((1, LANES), lambda i: (0, 0)),
            pl.BlockSpec((1, LANES), lambda i: (0, 0)),
        ],
        out_specs=[pl.BlockSpec((w, tm), lambda i: (0, i)) if fm else pl.BlockSpec((tm, w), row)
                   for w, _, fm in outs],
        out_shape=[jax.ShapeDtypeStruct((w, R) if fm else (R, w), dt) for w, dt, fm in outs],
        compiler_params=_cparams("arbitrary"),
        name="proj",
    )(xall, mod, g, w_in, cos_t, sa_t, sb_t, qn2, kn2)


CONV_HALO = 16
CONV_ROWS = 32


def _seq_edges(i, tile, n_lat_rows, S, Lc):
    r0 = i * tile
    in_lat = r0 < n_lat_rows
    pos = jnp.where(in_lat, r0 % S, (r0 - n_lat_rows) % Lc)
    length = jnp.where(in_lat, S, Lc)
    return pos == 0, pos + tile == length


def _conv_kernel(dims, tc, z_ref, zp_ref, zn_ref, w_ref, b_ref, lg_ref, lb_ref, o_ref, buf):
    B, S, Lc, R = dims
    first, last = _seq_edges(pl.program_id(0), tc, B * S, S, Lc)
    rows = tc + 2 * CONV_HALO
    zpad = jnp.concatenate([jnp.where(first, 0.0, zp_ref[...]), z_ref[...],
                            jnp.where(last, 0.0, zn_ref[...])], axis=0)
    buf[0] = zpad
    for r in range(1, SUBLANES):
        buf[r] = pltpu.roll(zpad, rows - r, 0)
    pad = (CONV_WIDTH - 1) // 2
    bias, lg, lb = b_ref[...], lg_ref[...], lb_ref[...]

    grp = CONV_ROWS // SUBLANES
    for c in range(tc // CONV_ROWS):
        r0 = c * CONV_ROWS
        acc = jnp.zeros((grp, SUBLANES, CONV_DIM), F32) + bias
        for k in range(CONV_WIDTH):
            off = CONV_HALO - pad + k
            base = r0 + (off // SUBLANES) * SUBLANES
            tile = buf[off % SUBLANES, base:base + CONV_ROWS, :].reshape(grp, SUBLANES, CONV_DIM)
            acc = acc + w_ref[k] * tile
        acc = acc.reshape(CONV_ROWS, CONV_DIM)
        mu = jnp.mean(acc, axis=-1, keepdims=True)
        xc = acc - mu
        var = jnp.mean(xc * xc, axis=-1, keepdims=True)
        y = xc * lax.rsqrt(var + EPS) * lg + lb
        o_ref[r0:r0 + CONV_ROWS, :] = _silu(y).astype(o_ref.dtype)


def _conv_call(dims, z, a_dw, a_dw_b, a_ln_g, a_ln_b, tc):
    B, S, Lc, R = dims
    assert S % tc == 0 and Lc % tc == 0
    hb = tc // CONV_HALO
    nhb = R // CONV_HALO
    return pl.pallas_call(
        functools.partial(_conv_kernel, dims, tc),
        grid=(R // tc,),
        in_specs=[
            pl.BlockSpec((tc, CONV_DIM), lambda i: (i, 0)),
            pl.BlockSpec((CONV_HALO, CONV_DIM), lambda i: (jnp.maximum(i * hb - 1, 0), 0)),
            pl.BlockSpec((CONV_HALO, CONV_DIM), lambda i: (jnp.minimum((i + 1) * hb, nhb - 1), 0)),
            pl.BlockSpec((CONV_WIDTH, SUBLANES, CONV_DIM), lambda i: (0, 0, 0)),
            pl.BlockSpec((1, CONV_DIM), lambda i: (0, 0)),
            pl.BlockSpec((1, CONV_DIM), lambda i: (0, 0)),
            pl.BlockSpec((1, CONV_DIM), lambda i: (0, 0)),
        ],
        out_specs=pl.BlockSpec((tc, CONV_DIM), lambda i: (i, 0)),
        out_shape=jax.ShapeDtypeStruct((R, CONV_DIM), BF16),
        scratch_shapes=[pltpu.VMEM((SUBLANES, tc + 2 * CONV_HALO, CONV_DIM), F32)],
        compiler_params=_cparams("arbitrary"),
        name="conv_branch",
    )(z, z, z, jnp.broadcast_to(a_dw[:, None, :], (CONV_WIDTH, SUBLANES, CONV_DIM)),
      a_dw_b.reshape(1, -1), a_ln_g.reshape(1, -1), a_ln_b.reshape(1, -1))


def _stacked_q_weights(qT, kv_heads, grp):
    zeros = jnp.zeros((HEAD_DIM, grp * BLOCK), BF16)
    out = []
    for kvh in range(kv_heads):
        piece = jnp.concatenate([qT[(kvh * grp + g) * HEAD_DIM:(kvh * grp + g + 1) * HEAD_DIM, :]
                                 for g in range(grp)], axis=1)
        out.append(jnp.concatenate([piece, zeros] if kvh == 0 else [zeros, piece], axis=0))
    return out


def _unstack_heads(oT, kv_heads, grp):
    OT = jnp.concatenate([oT[kvh][:, g * BLOCK:(g + 1) * BLOCK]
                          for kvh in range(kv_heads) for g in range(grp)], axis=0)
    return OT.T


WIN_PAIR = 2


def _window_kernel(dims, qT_ref, kp_ref, km_ref, kn_ref, kx_ref, vp_ref, vm_ref, vn_ref, vx_ref,
                   sink_ref, o_ref):
    B, S, Lc, R = dims
    nqb = S // BLOCK
    jj = pl.program_id(1)
    is_lat = jj < nqb // WIN_PAIR
    grp = WIN_HEADS // WIN_KV_HEADS
    nk = 3 * BLOCK + Lc

    key = lax.broadcasted_iota(jnp.int32, (nk, BLOCK), 0)
    qry = lax.broadcasted_iota(jnp.int32, (nk, BLOCK), 1)
    d = key - qry
    in_band = (d >= BLOCK - WINDOW) & (d <= BLOCK + WINDOW)
    km, vm, kx, vx = km_ref[...], vm_ref[...], kx_ref[...], vx_ref[...]
    subs = []
    for s in range(WIN_PAIR):
        j = jj * WIN_PAIR + s
        lo = jnp.where(j >= 1, 0, BLOCK)
        hi = jnp.where(is_lat, jnp.where(j <= nqb - 2, 3 * BLOCK, 2 * BLOCK), 0)
        bias = jnp.where((in_band & (key >= lo) & (key < hi)) | (key >= 3 * BLOCK), 0.0, NEG_INF)
        bias = jnp.concatenate([bias] * grp, axis=1)
        if s == 0:
            kk = jnp.concatenate([kp_ref[...], km, kx], axis=0)
            vT = jnp.concatenate([vp_ref[...], vm, vx], axis=1)
        else:
            kk = jnp.concatenate([km, kn_ref[...], kx], axis=0)
            vT = jnp.concatenate([vm, vn_ref[...], vx], axis=1)
        W = _stacked_q_weights(qT_ref[:, s * BLOCK:(s + 1) * BLOCK], WIN_KV_HEADS, grp)
        subs.append((bias, kk, vT, W))

    scores = [[jnp.dot(kk, W[kvh], preferred_element_type=F32) for kvh in range(WIN_KV_HEADS)]
              for (_, kk, _, W) in subs]
    probs = []
    for s, (bias, _, _, _) in enumerate(subs):
        row = []
        for kvh in range(WIN_KV_HEADS):
            sk = jnp.concatenate([jnp.full((1, BLOCK), sink_ref[kvh * grp + g] * LOG2E, F32)
                                  for g in range(grp)], axis=1)
            sT = scores[s][kvh] + bias
            m = jnp.maximum(jnp.max(sT, axis=0, keepdims=True), sk)
            pT = jnp.exp2(sT - m)
            l = jnp.sum(pT, axis=0, keepdims=True) + jnp.exp2(sk - m)
            row.append((pT.astype(BF16), 1.0 / l))
        probs.append(row)
    for s, (_, _, vT, _) in enumerate(subs):
        oT = [jnp.dot(vT[kvh * HEAD_DIM:(kvh + 1) * HEAD_DIM, :], probs[s][kvh][0],
                      preferred_element_type=F32) * probs[s][kvh][1] for kvh in range(WIN_KV_HEADS)]
        o_ref[s * BLOCK:(s + 1) * BLOCK, :] = _unstack_heads(oT, WIN_KV_HEADS, grp).astype(o_ref.dtype)


def _window_call(dims, bqT, bk, bvT, sink, ctx_queries):
    B, S, Lc, R = dims
    nqb, ncb = S // BLOCK, Lc // BLOCK
    assert nqb % WIN_PAIR == 0 and ncb % WIN_PAIR == 0
    npl, npc = nqb // WIN_PAIR, ncb // WIN_PAIR
    n_steps = npl + (npc if ctx_queries else 0)
    kvw = WIN_KV_HEADS * HEAD_DIM
    qw = WIN_HEADS * HEAD_DIM
    pair_rows = WIN_PAIR * BLOCK

    def q_blk(b, j):
        return jnp.where(j < npl, b * npl + j, B * npl + b * npc + (j - npl))

    def mid(b, j):
        return b * npl + jnp.minimum(j, npl - 1)

    def prev(b, j):
        return b * nqb + jnp.clip(j * WIN_PAIR - 1, 0, nqb - 1)

    def nxt(b, j):
        return b * nqb + jnp.clip(j * WIN_PAIR + WIN_PAIR, 0, nqb - 1)

    kspec = [pl.BlockSpec((BLOCK, kvw), lambda b, j: (prev(b, j), 0)),
             pl.BlockSpec((pair_rows, kvw), lambda b, j: (mid(b, j), 0)),
             pl.BlockSpec((BLOCK, kvw), lambda b, j: (nxt(b, j), 0)),
             pl.BlockSpec((Lc, kvw), lambda b, j: (B * S // Lc + b, 0))]
    vspec = [pl.BlockSpec((kvw, BLOCK), lambda b, j: (0, prev(b, j))),
             pl.BlockSpec((kvw, pair_rows), lambda b, j: (0, mid(b, j))),
             pl.BlockSpec((kvw, BLOCK), lambda b, j: (0, nxt(b, j))),
             pl.BlockSpec((kvw, Lc), lambda b, j: (0, B * S // Lc + b))]
    return pl.pallas_call(
        functools.partial(_window_kernel, dims),
        grid=(B, n_steps),
        in_specs=[pl.BlockSpec((qw, pair_rows), lambda b, j: (0, q_blk(b, j)))] + kspec + vspec
        + [pl.BlockSpec(memory_space=pltpu.SMEM)],
        out_specs=pl.BlockSpec((pair_rows, qw), lambda b, j: (q_blk(b, j), 0)),
        out_shape=jax.ShapeDtypeStruct((R, qw), BF16),
        compiler_params=_cparams("arbitrary", "arbitrary"),
        name="window_attn",
    )(bqT, bk, bk, bk, bk, bvT, bvT, bvT, bvT, sink)


GLB_TK = 128
GLB_ONLINE_TK = 256
GLB_MAX_SAFE_BOUND = 40.0


def _global_kernel(dims, par_ref, qT_ref, kl_ref, kx_ref, vTl_ref, vTx_ref, o_ref):
    B, S, Lc, R = dims
    nqb = S // BLOCK
    is_lat = pl.program_id(1) < nqb
    grp = GLB_HEADS // GLB_KV_HEADS
    cols = grp * BLOCK
    shift = par_ref[0]
    W = _stacked_q_weights(qT_ref[...], GLB_KV_HEADS, grp)

    def scores(kk):
        return tuple(jnp.dot(kk, W[kvh], preferred_element_type=F32) for kvh in range(GLB_KV_HEADS))

    def v_rows(vT, kvh):
        return vT[kvh * HEAD_DIM:(kvh + 1) * HEAD_DIM, :]

    def bounded(chunks):
        ls = [jnp.zeros((1, cols), F32)] * GLB_KV_HEADS
        accs = [jnp.zeros((HEAD_DIM, cols), F32)] * GLB_KV_HEADS
        s_cur = scores(chunks[0][0][chunks[0][2]:chunks[0][2] + GLB_TK, :])
        for i, (k_ref, v_ref, r0) in enumerate(chunks):
            s_next = None
            if i + 1 < len(chunks):
                nk, _, nr = chunks[i + 1]
                s_next = scores(nk[nr:nr + GLB_TK, :])
            vT = v_ref[:, r0:r0 + GLB_TK]
            for kvh in range(GLB_KV_HEADS):
                pT = jnp.exp2(s_cur[kvh] - shift)
                ls[kvh] = ls[kvh] + jnp.sum(pT, axis=0, keepdims=True)
                accs[kvh] = accs[kvh] + jnp.dot(v_rows(vT, kvh), pT.astype(BF16), preferred_element_type=F32)
            s_cur = s_next
        return tuple(acc * (1.0 / l) for l, acc in zip(ls, accs))

    def online():
        def update(carry, kk, vT):
            new = []
            for kvh, sT in enumerate(scores(kk)):
                m, l, acc = carry[kvh]
                m_new = jnp.maximum(m, jnp.max(sT, axis=0, keepdims=True))
                alpha = jnp.exp2(m - m_new)
                pT = jnp.exp2(sT - m_new)
                l = alpha * l + jnp.sum(pT, axis=0, keepdims=True)
                acc = alpha * acc + jnp.dot(v_rows(vT, kvh), pT.astype(BF16), preferred_element_type=F32)
                new.append((m_new, l, acc))
            return tuple(new)

        init = tuple((jnp.full((1, cols), NEG_INF, F32), jnp.zeros((1, cols), F32),
                      jnp.zeros((HEAD_DIM, cols), F32)) for _ in range(GLB_KV_HEADS))
        carry = init
        for r0 in range(0, Lc, GLB_ONLINE_TK):
            carry = update(carry, kx_ref[r0:r0 + GLB_ONLINE_TK, :], vTx_ref[:, r0:r0 + GLB_ONLINE_TK])

        def body(c, carry):
            r0 = pl.multiple_of(c * GLB_ONLINE_TK, GLB_ONLINE_TK)
            return update(carry, kl_ref[pl.ds(r0, GLB_ONLINE_TK), :], vTl_ref[:, pl.ds(r0, GLB_ONLINE_TK)])

        carry = lax.fori_loop(0, jnp.where(is_lat, S // GLB_ONLINE_TK, 0), body, carry)
        return tuple(acc * (1.0 / l) for (m, l, acc) in carry)

    ctx_chunks = [(kx_ref, vTx_ref, r0) for r0 in range(0, Lc, GLB_TK)]
    lat_chunks = [(kl_ref, vTl_ref, r0) for r0 in range(0, S, GLB_TK)]

    def bounded_any():
        return lax.cond(is_lat, lambda: bounded(ctx_chunks + lat_chunks), lambda: bounded(ctx_chunks))

    oT = lax.cond(par_ref[1] > 0.5, bounded_any, online)
    o_ref[...] = _unstack_heads(oT, GLB_KV_HEADS, grp).astype(o_ref.dtype)


def _global_call(dims, par, dqT, dk, dvT, ctx_queries):
    B, S, Lc, R = dims
    nqb, ncb = S // BLOCK, Lc // BLOCK
    n_steps = nqb + (ncb if ctx_queries else 0)
    lat_blocks = B * nqb
    kvw = GLB_KV_HEADS * HEAD_DIM
    qw = GLB_HEADS * HEAD_DIM
    assert Lc % GLB_ONLINE_TK == 0 and S % GLB_ONLINE_TK == 0 and GLB_ONLINE_TK % GLB_TK == 0

    def q_blk(b, j):
        return jnp.where(j < nqb, b * nqb + j, lat_blocks + b * ncb + (j - nqb))

    return pl.pallas_call(
        functools.partial(_global_kernel, dims),
        grid=(B, n_steps),
        in_specs=[pl.BlockSpec(memory_space=pltpu.SMEM),
                  pl.BlockSpec((qw, BLOCK), lambda b, j: (0, q_blk(b, j))),
                  pl.BlockSpec((S, kvw), lambda b, j: (b, 0)),
                  pl.BlockSpec((Lc, kvw), lambda b, j: (B * S // Lc + b, 0)),
                  pl.BlockSpec((kvw, S), lambda b, j: (0, b)),
                  pl.BlockSpec((kvw, Lc), lambda b, j: (0, B * S // Lc + b))],
        out_specs=pl.BlockSpec((BLOCK, qw), lambda b, j: (q_blk(b, j), 0)),
        out_shape=jax.ShapeDtypeStruct((R, qw), BF16),
        compiler_params=_cparams("arbitrary", "arbitrary"),
        name="global_attn",
    )(par, dqT, dk, dk, dvT, dvT)


RET_HPS = 2
RET_KERNEL_CHUNK = 256


def _ret_kernel(dims, ql_ref, qx_ref, kl_ref, kx_ref, vl_ref, vx_ref, gl_ref, gx_ref, lg_ref, gn_ref,
                ol_ref, ox_ref, fl_buf, fx_buf):
    B, S, Lc, R = dims
    C = RET_KERNEL_CHUNK
    assert S % C == 0 and Lc % C == 0
    hp = pl.program_id(1)
    pos_r = lax.broadcasted_iota(jnp.int32, (C, C), 0).astype(F32)
    pos_c = lax.broadcasted_iota(jnp.int32, (C, C), 1).astype(F32)
    diff = pos_r - pos_c
    col_pos = lax.broadcasted_iota(jnp.int32, (C, 1), 0).astype(F32)

    tabs = {}
    for dr in range(2):
        for hh in range(RET_HPS):
            lg = lg_ref[dr, hp * RET_HPS + hh]
            dd = diff if dr == 0 else -diff
            dmat = jnp.where(dd >= 0, jnp.exp(lg * jnp.maximum(dd, 0.0)), 0.0)
            p_eff = col_pos if dr == 0 else (C - 1.0) - col_pos
            kdec = jnp.exp(lg * ((C - 1.0) - p_eff))
            qdec = jnp.exp(lg * (p_eff + 1.0))
            cdec = jnp.exp(jnp.zeros((1, RET_V_DIM), F32) + lg * float(C))
            tabs[dr, hh] = (dmat, kdec, qdec, cdec)

    qk_w = RET_HPS * RET_QK_DIM
    lane = lax.broadcasted_iota(jnp.int32, (1, qk_w), 1)
    row = lax.broadcasted_iota(jnp.int32, (qk_w, 1), 0)
    lane_m = [jnp.where((lane >= hh * RET_QK_DIM) & (lane < (hh + 1) * RET_QK_DIM), 1.0, 0.0)
              for hh in range(RET_HPS)]
    row_m = [jnp.where((row >= hh * RET_QK_DIM) & (row < (hh + 1) * RET_QK_DIM), 1.0, 0.0)
             for hh in range(RET_HPS)]

    def step(q_ref, k_ref, v_ref, r0, dr, hh, state):
        dmat, kdec, qdec, cdec = tabs[dr, hh]
        vs = slice(hh * RET_V_DIM, (hh + 1) * RET_V_DIM)
        qf = q_ref[pl.ds(r0, C), :].astype(F32) * lane_m[hh]
        k2 = k_ref[pl.ds(r0, C), :]
        vc = v_ref[pl.ds(r0, C), vs]
        sc = lax.dot_general(qf.astype(BF16), k2, (((1,), (1,)), ((), ())), preferred_element_type=F32) * dmat
        inner = jnp.dot(sc.astype(BF16), vc, preferred_element_type=F32)
        qd = (qf * qdec).astype(BF16)
        cross = jnp.dot(qd, state.astype(BF16), preferred_element_type=F32)
        kd = (k2.astype(F32) * kdec).astype(BF16)
        u = lax.dot_general(kd, vc, (((0,), (0,)), ((), ())), preferred_element_type=F32)
        return inner + cross, cdec * state + u * row_m[hh]

    def sweep(q_ref, k_ref, v_ref, f_buf, n_chunks, states):
        def body(i, st):
            st = list(st)
            rf = pl.multiple_of(i * C, C)
            rb = pl.multiple_of((n_chunks - 1 - i) * C, C)
            for hh in range(RET_HPS):
                vs = slice(hh * RET_V_DIM, (hh + 1) * RET_V_DIM)
                of, st[hh] = step(q_ref, k_ref, v_ref, rf, 0, hh, st[hh])
                f_buf[0, pl.ds(rf, C), vs] = of
                ob, st[RET_HPS + hh] = step(q_ref, k_ref, v_ref, rb, 1, hh, st[RET_HPS + hh])
                f_buf[1, pl.ds(rb, C), vs] = ob
            return tuple(st)
        return lax.fori_loop(0, n_chunks, body, states, unroll=2 if n_chunks % 2 == 0 else 1)

    def finish(f_buf, g_ref, o_ref, n_chunks):
        def body(i, carry):
            r0 = pl.multiple_of(i * C, C)
            for hh in range(RET_HPS):
                vs = slice(hh * RET_V_DIM, (hh + 1) * RET_V_DIM)
                o = f_buf[0, pl.ds(r0, C), vs] + f_buf[1, pl.ds(r0, C), vs]
                mu = jnp.mean(o, axis=-1, keepdims=True)
                xc = o - mu
                var = jnp.mean(xc * xc, axis=-1, keepdims=True)
                y = xc * lax.rsqrt(var + EPS) * gn_ref[:, vs]
                o_ref[pl.ds(r0, C), vs] = (_silu(g_ref[pl.ds(r0, C), vs]) * y).astype(o_ref.dtype)
            return carry
        lax.fori_loop(0, n_chunks, body, 0)

    zero = jnp.zeros((qk_w, RET_V_DIM), F32)
    states = sweep(qx_ref, kx_ref, vx_ref, fx_buf, Lc // C, (zero,) * (2 * RET_HPS))
    sweep(ql_ref, kl_ref, vl_ref, fl_buf, S // C, states)
    finish(fx_buf, gx_ref, ox_ref, Lc // C)
    finish(fl_buf, gl_ref, ol_ref, S // C)


def _ret_call(dims, cq, ck, cv, cg, log_gamma, gn_g):
    B, S, Lc, R = dims
    qw = RET_HPS * RET_QK_DIM
    vw = RET_HPS * RET_V_DIM
    nhp = RET_HEADS // RET_HPS
    xoff = B * S // Lc

    def lat(w):
        return pl.BlockSpec((S, w), lambda b, h: (b, h))

    def ctx(w):
        return pl.BlockSpec((Lc, w), lambda b, h: (xoff + b, h))

    return pl.pallas_call(
        functools.partial(_ret_kernel, dims),
        grid=(B, nhp),
        in_specs=[lat(qw), ctx(qw), lat(qw), ctx(qw), lat(vw), ctx(vw), lat(vw), ctx(vw),
                  pl.BlockSpec(memory_space=pltpu.SMEM),
                  pl.BlockSpec((1, vw), lambda b, h: (0, h))],
        out_specs=[pl.BlockSpec((S, vw), lambda b, h: (b, h)),
                   pl.BlockSpec((Lc, vw), lambda b, h: (b, h))],
        out_shape=[jax.ShapeDtypeStruct((B * S, RET_HEADS * RET_V_DIM), BF16),
                   jax.ShapeDtypeStruct((B * Lc, RET_HEADS * RET_V_DIM), BF16)],
        scratch_shapes=[pltpu.VMEM((2, S, vw), F32), pltpu.VMEM((2, Lc, vw), F32)],
        compiler_params=_cparams("arbitrary", "arbitrary"),
        name="retention",
    )(cq, cq, ck, ck, cv, cv, cg, cg, log_gamma, gn_g.reshape(1, -1))


MERGE_SUB = 256


def _merge_kernel(n_lat_tiles, ya_ref, yb_ref, yrl_ref, yrx_ref, yd_ref, gt_ref, x_ref, mod_ref, g_ref,
                  wbr_ref, wo_ref, o_ref):
    tm = x_ref.shape[0]
    is_lat = pl.program_id(0) < n_lat_tiles
    gate = mod_ref[0][:, 2 * D_MODEL:3 * D_MODEL]
    subs = [slice(r0, r0 + MERGE_SUB) for r0 in range(0, tm, MERGE_SUB)]

    def branch_dots(rs):
        yr = jnp.where(is_lat, yrl_ref[rs, :], yrx_ref[rs, :])
        return [jnp.dot(b, wbr_ref[i], preferred_element_type=F32)
                for i, b in enumerate((ya_ref[rs, :], yb_ref[rs, :], yr, yd_ref[rs, :]))]

    dots = [branch_dots(rs) for rs in subs]
    for rs, d in zip(subs, dots):
        acc = None
        for i in range(N_BRANCH):
            t = gt_ref[rs, i * D_MODEL:(i + 1) * D_MODEL].astype(F32) * d[i]
            acc = t if acc is None else acc + t
        y = jnp.dot(acc.astype(BF16), wo_ref[...], preferred_element_type=F32)
        r = y * lax.rsqrt(jnp.mean(y * y, axis=-1, keepdims=True) + EPS) * g_ref[...]
        o_ref[rs, :] = x_ref[rs, :] + gate * r


def _merge_call(dims, ya, yb, yr_l, yr_x, yd, gates, xall, mod, g, w_br, w_o, tm, rows_out):
    B, S, Lc, R = dims
    assert tm % MERGE_SUB == 0
    n_lat = B * S // tm
    n_ctx = B * Lc // tm

    def row(i):
        return (i, 0)

    br = pl.BlockSpec((tm, BRANCH_DIM), row)
    return pl.pallas_call(
        functools.partial(_merge_kernel, n_lat),
        grid=(rows_out // tm,),
        in_specs=[br, br,
                  pl.BlockSpec((tm, BRANCH_DIM), lambda i: (jnp.minimum(i, n_lat - 1), 0)),
                  pl.BlockSpec((tm, BRANCH_DIM), lambda i: (jnp.clip(i - n_lat, 0, n_ctx - 1), 0)),
                  br,
                  pl.BlockSpec((tm, N_BRANCH * D_MODEL), row),
                  pl.BlockSpec((tm, D_MODEL), row),
                  pl.BlockSpec((1, 1, 6 * D_MODEL), lambda i: (jnp.minimum(i * tm // S, B), 0, 0)),
                  pl.BlockSpec((1, D_MODEL), lambda i: (0, 0)),
                  _resident((N_BRANCH, BRANCH_DIM, D_MODEL)),
                  _resident((D_MODEL, D_MODEL))],
        out_specs=pl.BlockSpec((tm, D_MODEL), row),
        out_shape=jax.ShapeDtypeStruct((rows_out, D_MODEL), F32),
        compiler_params=_cparams("arbitrary"),
        name="merge",
    )(ya, yb, yr_l, yr_x, yd, gates, xall, mod, g, w_br, w_o)


FFN_HALO = 8
FFN_CN = 256
FFN_LOOKAHEAD = 3


def _ffn_kernel(dims, tm, x_ref, xp_ref, xn_ref, mod_ref, g2_ref, g3_ref, wup_ref, dw_ref, db_ref, wdn_ref,
                o_ref, acc_buf):
    B, S, Lc, R = dims
    first, last = _seq_edges(pl.program_id(0), tm, B * S, S, Lc)
    m = mod_ref[0]
    g2 = g2_ref[...]
    x = x_ref[...]
    hp = jnp.where(first, 0.0, _modulated(xp_ref[...], g2, m, 1))
    hn = jnp.where(last, 0.0, _modulated(xn_ref[...], g2, m, 1))
    h = jnp.concatenate([hp, _modulated(x, g2, m, 1), hn], axis=0).astype(BF16)

    rows = tm + 2 * FFN_HALO

    def up(j):
        return tuple(jnp.dot(h, wup_ref[:, c0:c0 + FFN_CN], preferred_element_type=F32)
                     for c0 in (j * FFN_CN, D_FF + j * FFN_CN))

    def conv3(u, col0):
        w = dw_ref[:, col0:col0 + FFN_CN]
        y = (w[0:1] * pltpu.roll(u, 1, 0) + w[1:2] * u + w[2:3] * pltpu.roll(u, rows - 1, 0)
             + db_ref[:, col0:col0 + FFN_CN])
        return y[FFN_HALO:FFN_HALO + tm]

    n_chunks = D_FF // FFN_CN
    ahead = [up(j) for j in range(FFN_LOOKAHEAD)]
    for j in range(n_chunks):
        ca, cb = j * FFN_CN, D_FF + j * FFN_CN
        if j + FFN_LOOKAHEAD < n_chunks:
            ahead.append(up(j + FFN_LOOKAHEAD))
        u_cur = ahead.pop(0)
        a, b = conv3(u_cur[0], ca), conv3(u_cur[1], cb)
        t = (_silu(a) * b).astype(BF16)
        part = jnp.dot(t, wdn_ref[ca:ca + FFN_CN, :], preferred_element_type=F32)
        if j == 0:
            acc_buf[...] = part
        else:
            acc_buf[...] += part

    y = acc_buf[...]
    r = y * lax.rsqrt(jnp.mean(y * y, axis=-1, keepdims=True) + EPS) * g3_ref[...]
    o_ref[...] = x + m[:, 5 * D_MODEL:6 * D_MODEL] * r


def _ffn_call(dims, xall, mod, g2, g3, f_up, f_dw, f_dw_b, f_down, tm):
    B, S, Lc, R = dims
    rows = xall.shape[0]
    assert rows in (R, B * S) and S % tm == 0 and Lc % tm == 0
    hb = tm // FFN_HALO
    nhb = rows // FFN_HALO
    return pl.pallas_call(
        functools.partial(_ffn_kernel, dims, tm),
        grid=(rows // tm,),
        in_specs=[
            pl.BlockSpec((tm, D_MODEL), lambda i: (i, 0)),
            pl.BlockSpec((FFN_HALO, D_MODEL), lambda i: (jnp.maximum(i * hb - 1, 0), 0)),
            pl.BlockSpec((FFN_HALO, D_MODEL), lambda i: (jnp.minimum((i + 1) * hb, nhb - 1), 0)),
            pl.BlockSpec((1, 1, 6 * D_MODEL), lambda i: (jnp.minimum(i * tm // S, B), 0, 0)),
            pl.BlockSpec((1, D_MODEL), lambda i: (0, 0)),
            pl.BlockSpec((1, D_MODEL), lambda i: (0, 0)),
            _resident((D_MODEL, 2 * D_FF)),
            pl.BlockSpec((FFN_CONV_WIDTH, 2 * D_FF), lambda i: (0, 0)),
            pl.BlockSpec((1, 2 * D_FF), lambda i: (0, 0)),
            _resident((D_FF, D_MODEL)),
        ],
        out_specs=pl.BlockSpec((tm, D_MODEL), lambda i: (i, 0)),
        out_shape=jax.ShapeDtypeStruct((rows, D_MODEL), F32),
        scratch_shapes=[pltpu.VMEM((tm, D_MODEL), F32)],
        compiler_params=_cparams("arbitrary"),
        name="conv_ffn",
    )(xall, xall, xall, mod, g2, g3, f_up, f_dw, f_dw_b.reshape(1, -1), f_down)


def _rope_tables(S, pad_rows):
    rows = S // GRID_W
    row = jnp.repeat(jnp.arange(rows), GRID_W).astype(F32)
    col = jnp.tile(jnp.arange(GRID_W), rows).astype(F32)
    half = HEAD_DIM // 2
    inv = ROPE_BASE ** (-jnp.arange(0, half, 2, dtype=F32) / half)
    ang = jnp.concatenate([row[:, None] * inv, col[:, None] * inv], axis=-1)
    cos, sin = jnp.cos(ang), jnp.sin(ang)
    zero = jnp.zeros_like(sin)
    cos_h = jnp.concatenate([cos, cos], axis=-1)
    sa_h = jnp.concatenate([-sin, zero], axis=-1)
    sb_h = jnp.concatenate([zero, sin], axis=-1)
    reps = LANES // HEAD_DIM

    def slab(t, fill):
        t = jnp.tile(t, (1, reps))
        return jnp.concatenate([t, jnp.full((pad_rows, LANES), fill, F32)], axis=0)

    return slab(cos_h, 1.0), slab(sa_h, 0.0), slab(sb_h, 0.0)


def kernel(x, c, ctx, c_ctx, ada_w, ada_b, norm_g, w_in, a_dw, a_dw_b, a_ln_g, a_ln_b, b_sink,
           c_decay_logit, c_gn_g, d_qn_g, d_kn_g, w_br, w_o, f_up, f_dw, f_dw_b, f_down):
    B, S, _ = x.shape
    Lc = ctx.shape[1]
    R = B * (S + Lc)
    dims = (B, S, Lc, R)
    tm = ROW_TILE
    assert B + 1 <= MOD_ROWS and S % tm == 0 and (B * Lc) % tm == 0 and Lc % BLOCK == 0 and S % Lc == 0

    cvec = jnp.concatenate([c, c_ctx[None, :], jnp.zeros((MOD_ROWS - B - 1, D_MODEL), F32)], axis=0)
    mods = _ada_call(cvec, ada_w, ada_b).reshape(DEPTH, MOD_ROWS, 1, 6 * D_MODEL)
    cos_t, sa_t, sb_t = _rope_tables(S, tm)
    reps = LANES // HEAD_DIM
    log_gamma = jnp.log(jax.nn.sigmoid(c_decay_logit.astype(F32)))

    xall = jnp.concatenate([x.reshape(B * S, D_MODEL), ctx.reshape(B * Lc, D_MODEL)], axis=0)
    for l in range(DEPTH):
        mod = mods[l]
        ng = norm_g[l].reshape(4, 1, D_MODEL)
        (z, bq, bk, bv, cq, ck, cv, cg, dq, dk, dv, gates) = _proj_call(
            dims, xall, mod, ng[0], w_in[l].astype(BF16), cos_t, sa_t, sb_t,
            jnp.tile(d_qn_g[l], reps)[None, :], jnp.tile(d_kn_g[l], reps)[None, :], tm)
        with_ctx = l < DEPTH - 1
        ya = _conv_call(dims, z, a_dw[l], a_dw_b[l], a_ln_g[l], a_ln_b[l], 256)
        yb = _window_call(dims, bq, bk, bv, b_sink[l], with_ctx)
        yr_l, yr_x = _ret_call(dims, cq, ck, cv, cg, log_gamma[l], c_gn_g[l])
        bound = 1.02 * HEAD_DIM * (HEAD_DIM ** -0.5) * jnp.max(jnp.abs(d_qn_g[l])) * jnp.max(jnp.abs(d_kn_g[l]))
        par = jnp.stack([bound * LOG2E, (bound <= GLB_MAX_SAFE_BOUND).astype(F32)])
        yd = _global_call(dims, par, dq, dk, dv, with_ctx)
        rows_out = R if with_ctx else B * S
        xall = _merge_call(dims, ya, yb, yr_l, yr_x, yd, gates, xall, mod, ng[1],
                           w_br[l].astype(BF16), w_o[l].astype(BF16), 2 * MERGE_SUB, rows_out)
        xall = _ffn_call(dims, xall, mod, ng[2], ng[3], f_up[l].astype(BF16), f_dw[l], f_dw_b[l],
                         f_down[l].astype(BF16), tm)
    return xall.reshape(B, S, D_MODEL)
```

```python
import functools

import jax
import jax.numpy as jnp
import numpy as np
from jax import lax
from jax.experimental import pallas as pl
from jax.experimental.pallas import tpu as pltpu

D_MODEL = 1024
DEPTH = 4
GRID_W = 64
HEAD_DIM = 64
BLOCK = 128
ROPE_BASE = 10000.0
EPS = 1e-6
NEG_INF = -1e30
CONV_DIM = 512
CONV_WIDTH = 31
WIN_HEADS = 8
WIN_KV_HEADS = 2
WINDOW = 128
RET_HEADS = 4
RET_QK_DIM = 64
RET_V_DIM = 128
RET_CHUNK = 128
GLB_HEADS = 8
GLB_KV_HEADS = 2
N_BRANCH = 4
BRANCH_DIM = 512
D_FF = 2816
FFN_CONV_WIDTH = 3

IN_SIZES = (2 * CONV_DIM,
            WIN_HEADS * HEAD_DIM, WIN_KV_HEADS * HEAD_DIM, WIN_KV_HEADS * HEAD_DIM,
            RET_HEADS * RET_QK_DIM, RET_HEADS * RET_QK_DIM, RET_HEADS * RET_V_DIM, RET_HEADS * RET_V_DIM,
            GLB_HEADS * HEAD_DIM, GLB_KV_HEADS * HEAD_DIM, GLB_KV_HEADS * HEAD_DIM,
            N_BRANCH * D_MODEL)
IN_DIM = sum(IN_SIZES)
IN_OFF = tuple(int(v) for v in np.cumsum((0,) + IN_SIZES)[:-1])
(OFF_AU, OFF_BQ, OFF_BK, OFF_BV, OFF_CQ, OFF_CK, OFF_CV, OFF_CG,
 OFF_DQ, OFF_DK, OFF_DV, OFF_GATES) = IN_OFF

LANES = 128
SUBLANES = 8
MOD_ROWS = 16
ROW_TILE = 256
VMEM_LIMIT = 56 * 1024 * 1024
F32 = jnp.float32
BF16 = jnp.bfloat16
LOG2E = 1.4426950408889634


def _sigmoid(v):
    return 1.0 / (1.0 + jnp.exp(-v))


def _silu(v):
    return v * _sigmoid(v)


def _cparams(*sem):
    return pltpu.CompilerParams(dimension_semantics=sem, vmem_limit_bytes=VMEM_LIMIT)


def _resident(shape):
    nd = len(shape)
    return pl.BlockSpec(shape, lambda *_: (0,) * nd, pipeline_mode=pl.Buffered(1))


def _ada_kernel(c_ref, w_ref, b_ref, o_ref):
    a = _silu(c_ref[...]).astype(BF16)
    w = w_ref[0].astype(BF16)
    o_ref[0] = jnp.dot(a, w, preferred_element_type=F32) + b_ref[0]


def _ada_call(cvec, ada_w, ada_b):
    tn = 1536
    n6 = 6 * D_MODEL
    return pl.pallas_call(
        _ada_kernel,
        grid=(DEPTH, n6 // tn),
        in_specs=[
            pl.BlockSpec((MOD_ROWS, D_MODEL), lambda l, j: (0, 0)),
            pl.BlockSpec((1, D_MODEL, tn), lambda l, j: (l, 0, j)),
            pl.BlockSpec((1, 1, tn), lambda l, j: (l, 0, j)),
        ],
        out_specs=pl.BlockSpec((1, MOD_ROWS, tn), lambda l, j: (l, 0, j)),
        out_shape=jax.ShapeDtypeStruct((DEPTH, MOD_ROWS, n6), F32),
        compiler_params=_cparams("arbitrary", "arbitrary"),
        name="ada_mod",
    )(cvec, ada_w, ada_b.reshape(DEPTH, 1, n6))


def _modulated(x, g, m, which):
    o = 3 * which * D_MODEL
    y = x * lax.rsqrt(jnp.mean(x * x, axis=-1, keepdims=True) + EPS) * g
    return y * (1.0 + m[:, o + D_MODEL:o + 2 * D_MODEL]) + m[:, o:o + D_MODEL]


def _rope128(v, cos, sa, sb):
    return v * cos + pltpu.roll(v, LANES - HEAD_DIM // 2, 1) * sa + pltpu.roll(v, HEAD_DIM // 2, 1) * sb


def _head_rms128(v, gain):
    lane = lax.broadcasted_iota(jnp.int32, (1, LANES), 1)
    lo = lane < HEAD_DIM
    sq = v * v
    s0 = jnp.sum(jnp.where(lo, sq, 0.0), axis=-1, keepdims=True)
    s1 = jnp.sum(jnp.where(lo, 0.0, sq), axis=-1, keepdims=True)
    ms = jnp.where(lo, s0, s1) * (1.0 / HEAD_DIM)
    return v * lax.rsqrt(ms + EPS) * gain


def _proj_kernel(x_ref, mod_ref, g_ref, w_ref, cos_ref, sa_ref, sb_ref, qn_ref, kn_ref,
                 z_ref, bq_ref, bk_ref, bv_ref, cq_ref, ck_ref, cv_ref, cg_ref,
                 dq_ref, dk_ref, dv_ref, gt_ref):
    h = _modulated(x_ref[...], g_ref[...], mod_ref[0], 0).astype(BF16)

    def proj(off, width):
        return jnp.dot(h, w_ref[:, off:off + width], preferred_element_type=F32)

    u = proj(OFF_AU, 2 * CONV_DIM)
    z_ref[...] = u[:, :CONV_DIM] * _sigmoid(u[:, CONV_DIM:])

    cos, sa, sb = cos_ref[...], sa_ref[...], sb_ref[...]
    scale = HEAD_DIM ** -0.5

    def roped(off, width, out_ref, gain_ref, mul, feature_major=False):
        v = proj(off, width)
        for c in range(width // LANES):
            s = v[:, c * LANES:(c + 1) * LANES]
            if gain_ref is not None:
                s = _head_rms128(s, gain_ref[...])
            s = _rope128(s, cos, sa, sb) * mul
            if feature_major:
                out_ref[c * LANES:(c + 1) * LANES, :] = s.T.astype(out_ref.dtype)
            else:
                out_ref[:, c * LANES:(c + 1) * LANES] = s.astype(out_ref.dtype)

    roped(OFF_BQ, WIN_HEADS * HEAD_DIM, bq_ref, None, scale * LOG2E, feature_major=True)
    roped(OFF_BK, WIN_KV_HEADS * HEAD_DIM, bk_ref, None, 1.0)
    bv_ref[...] = proj(OFF_BV, WIN_KV_HEADS * HEAD_DIM).astype(BF16).T

    cq_ref[...] = proj(OFF_CQ, RET_HEADS * RET_QK_DIM).astype(BF16)
    ck_ref[...] = (proj(OFF_CK, RET_HEADS * RET_QK_DIM) * (RET_QK_DIM ** -0.5)).astype(BF16)
    cv_ref[...] = proj(OFF_CV, RET_HEADS * RET_V_DIM).astype(BF16)
    cg_ref[...] = proj(OFF_CG, RET_HEADS * RET_V_DIM)

    roped(OFF_DQ, GLB_HEADS * HEAD_DIM, dq_ref, qn_ref, scale * LOG2E, feature_major=True)
    roped(OFF_DK, GLB_KV_HEADS * HEAD_DIM, dk_ref, kn_ref, 1.0)
    dv_ref[...] = proj(OFF_DV, GLB_KV_HEADS * HEAD_DIM).astype(BF16).T

    for i in range(N_BRANCH):
        gl = proj(OFF_GATES + i * D_MODEL, D_MODEL)
        gt_ref[:, i * D_MODEL:(i + 1) * D_MODEL] = _sigmoid(gl).astype(BF16)


def _proj_call(dims, xall, mod, g, w_in, cos_t, sa_t, sb_t, qn2, kn2, tm):
    B, S, Lc, R = dims
    nlat = B * S // tm
    n_rope_lat = S // tm

    def row(i):
        return (i, 0)

    def mod_idx(i):
        return (jnp.minimum(i * tm // S, B), 0, 0)

    def rope_idx(i):
        return (jnp.where(i < nlat, i % n_rope_lat, n_rope_lat), 0)

    outs = [(CONV_DIM, F32, False), (512, BF16, True), (128, BF16, False), (128, BF16, True),
            (256, BF16, False), (256, BF16, False), (512, BF16, False), (512, F32, False),
            (512, BF16, True), (128, BF16, False), (128, BF16, True), (4 * D_MODEL, BF16, False)]
    return pl.pallas_call(
        _proj_kernel,
        grid=(R // tm,),
        in_specs=[
            pl.BlockSpec((tm, D_MODEL), row),
            pl.BlockSpec((1, 1, 6 * D_MODEL), mod_idx),
            pl.BlockSpec((1, D_MODEL), lambda i: (0, 0)),
            _resident((D_MODEL, IN_DIM)),
            pl.BlockSpec((tm, LANES), rope_idx),
            pl.BlockSpec((tm, LANES), rope_idx),
            pl.BlockSpec((tm, LANES), rope_idx),
            pl.BlockSpec((1, LANES), lambda i: (0, 0)),
            pl.BlockSpec((1, LANES), lambda i: (0, 0)),
        ],
        out_specs=[pl.BlockSpec((w, tm), lambda i: (0, i)) if fm else pl.BlockSpec((tm, w), row)
                   for w, _, fm in outs],
        out_shape=[jax.ShapeDtypeStruct((w, R) if fm else (R, w), dt) for w, dt, fm in outs],
        compiler_params=_cparams("arbitrary"),
        name="proj",
    )(xall, mod, g, w_in, cos_t, sa_t, sb_t, qn2, kn2)


CONV_HALO = 16
CONV_ROWS = 32


def _seq_edges(i, tile, n_lat_rows, S, Lc):
    r0 = i * tile
    in_lat = r0 < n_lat_rows
    pos = jnp.where(in_lat, r0 % S, (r0 - n_lat_rows) % Lc)
    length = jnp.where(in_lat, S, Lc)
    return pos == 0, pos + tile == length


def _conv_kernel(dims, tc, z_ref, zp_ref, zn_ref, w_ref, b_ref, lg_ref, lb_ref, o_ref, buf):
    B, S, Lc, R = dims
    first, last = _seq_edges(pl.program_id(0), tc, B * S, S, Lc)
    rows = tc + 2 * CONV_HALO
    zpad = jnp.concatenate([jnp.where(first, 0.0, zp_ref[...]), z_ref[...],
                            jnp.where(last, 0.0, zn_ref[...])], axis=0)
    buf[0] = zpad
    for r in range(1, SUBLANES):
        buf[r] = pltpu.roll(zpad, rows - r, 0)
    pad = (CONV_WIDTH - 1) // 2
    bias, lg, lb = b_ref[...], lg_ref[...], lb_ref[...]

    grp = CONV_ROWS // SUBLANES
    for c in range(tc // CONV_ROWS):
        r0 = c * CONV_ROWS
        acc = jnp.zeros((grp, SUBLANES, CONV_DIM), F32) + bias
        for k in range(CONV_WIDTH):
            off = CONV_HALO - pad + k
            base = r0 + (off // SUBLANES) * SUBLANES
            tile = buf[off % SUBLANES, base:base + CONV_ROWS, :].reshape(grp, SUBLANES, CONV_DIM)
            acc = acc + w_ref[k] * tile
        acc = acc.reshape(CONV_ROWS, CONV_DIM)
        mu = jnp.mean(acc, axis=-1, keepdims=True)
        xc = acc - mu
        var = jnp.mean(xc * xc, axis=-1, keepdims=True)
        y = xc * lax.rsqrt(var + EPS) * lg + lb
        o_ref[r0:r0 + CONV_ROWS, :] = _silu(y).astype(o_ref.dtype)


def _conv_call(dims, z, a_dw, a_dw_b, a_ln_g, a_ln_b, tc):
    B, S, Lc, R = dims
    assert S % tc == 0 and Lc % tc == 0
    hb = tc // CONV_HALO
    nhb = R // CONV_HALO
    return pl.pallas_call(
        functools.partial(_conv_kernel, dims, tc),
        grid=(R // tc,),
        in_specs=[
            pl.BlockSpec((tc, CONV_DIM), lambda i: (i, 0)),
            pl.BlockSpec((CONV_HALO, CONV_DIM), lambda i: (jnp.maximum(i * hb - 1, 0), 0)),
            pl.BlockSpec((CONV_HALO, CONV_DIM), lambda i: (jnp.minimum((i + 1) * hb, nhb - 1), 0)),
            pl.BlockSpec((CONV_WIDTH, SUBLANES, CONV_DIM), lambda i: (0, 0, 0)),
            pl.BlockSpec((1, CONV_DIM), lambda i: (0, 0)),
            pl.BlockSpec((1, CONV_DIM), lambda i: (0, 0)),
            pl.BlockSpec((1, CONV_DIM), lambda i: (0, 0)),
        ],
        out_specs=pl.BlockSpec((tc, CONV_DIM), lambda i: (i, 0)),
        out_shape=jax.ShapeDtypeStruct((R, CONV_DIM), BF16),
        scratch_shapes=[pltpu.VMEM((SUBLANES, tc + 2 * CONV_HALO, CONV_DIM), F32)],
        compiler_params=_cparams("arbitrary"),
        name="conv_branch",
    )(z, z, z, jnp.broadcast_to(a_dw[:, None, :], (CONV_WIDTH, SUBLANES, CONV_DIM)),
      a_dw_b.reshape(1, -1), a_ln_g.reshape(1, -1), a_ln_b.reshape(1, -1))


def _stacked_q_weights(qT, kv_heads, grp):
    zeros = jnp.zeros((HEAD_DIM, grp * BLOCK), BF16)
    out = []
    for kvh in range(kv_heads):
        piece = jnp.concatenate([qT[(kvh * grp + g) * HEAD_DIM:(kvh * grp + g + 1) * HEAD_DIM, :]
                                 for g in range(grp)], axis=1)
        out.append(jnp.concatenate([piece, zeros] if kvh == 0 else [zeros, piece], axis=0))
    return out


def _unstack_heads(oT, kv_heads, grp):
    OT = jnp.concatenate([oT[kvh][:, g * BLOCK:(g + 1) * BLOCK]
                          for kvh in range(kv_heads) for g in range(grp)], axis=0)
    return OT.T


WIN_PAIR = 2


def _window_kernel(dims, qT_ref, kp_ref, km_ref, kn_ref, kx_ref, vp_ref, vm_ref, vn_ref, vx_ref,
                   sink_ref, o_ref):
    B, S, Lc, R = dims
    nqb = S // BLOCK
    jj = pl.program_id(1)
    is_lat = jj < nqb // WIN_PAIR
    grp = WIN_HEADS // WIN_KV_HEADS
    nk = 3 * BLOCK + Lc

    key = lax.broadcasted_iota(jnp.int32, (nk, BLOCK), 0)
    qry = lax.broadcasted_iota(jnp.int32, (nk, BLOCK), 1)
    d = key - qry
    in_band = (d >= BLOCK - WINDOW) & (d <= BLOCK + WINDOW)
    km, vm, kx, vx = km_ref[...], vm_ref[...], kx_ref[...], vx_ref[...]
    subs = []
    for s in range(WIN_PAIR):
        j = jj * WIN_PAIR + s
        lo = jnp.where(j >= 1, 0, BLOCK)
        hi = jnp.where(is_lat, jnp.where(j <= nqb - 2, 3 * BLOCK, 2 * BLOCK), 0)
        bias = jnp.where((in_band & (key >= lo) & (key < hi)) | (key >= 3 * BLOCK), 0.0, NEG_INF)
        bias = jnp.concatenate([bias] * grp, axis=1)
        if s == 0:
            kk = jnp.concatenate([kp_ref[...], km, kx], axis=0)
            vT = jnp.concatenate([vp_ref[...], vm, vx], axis=1)
        else:
            kk = jnp.concatenate([km, kn_ref[...], kx], axis=0)
            vT = jnp.concatenate([vm, vn_ref[...], vx], axis=1)
        W = _stacked_q_weights(qT_ref[:, s * BLOCK:(s + 1) * BLOCK], WIN_KV_HEADS, grp)
        subs.append((bias, kk, vT, W))

    scores = [[jnp.dot(kk, W[kvh], preferred_element_type=F32) for kvh in range(WIN_KV_HEADS)]
              for (_, kk, _, W) in subs]
    probs = []
    for s, (bias, _, _, _) in enumerate(subs):
        row = []
        for kvh in range(WIN_KV_HEADS):
            sk = jnp.concatenate([jnp.full((1, BLOCK), sink_ref[kvh * grp + g] * LOG2E, F32)
                                  for g in range(grp)], axis=1)
            sT = scores[s][kvh] + bias
            m = jnp.maximum(jnp.max(sT, axis=0, keepdims=True), sk)
            pT = jnp.exp2(sT - m)
            l = jnp.sum(pT, axis=0, keepdims=True) + jnp.exp2(sk - m)
            row.append((pT.astype(BF16), 1.0 / l))
        probs.append(row)
    for s, (_, _, vT, _) in enumerate(subs):
        oT = [jnp.dot(vT[kvh * HEAD_DIM:(kvh + 1) * HEAD_DIM, :], probs[s][kvh][0],
                      preferred_element_type=F32) * probs[s][kvh][1] for kvh in range(WIN_KV_HEADS)]
        o_ref[s * BLOCK:(s + 1) * BLOCK, :] = _unstack_heads(oT, WIN_KV_HEADS, grp).astype(o_ref.dtype)


def _window_call(dims, bqT, bk, bvT, sink, ctx_queries):
    B, S, Lc, R = dims
    nqb, ncb = S // BLOCK, Lc // BLOCK
    assert nqb % WIN_PAIR == 0 and ncb % WIN_PAIR == 0
    npl, npc = nqb // WIN_PAIR, ncb // WIN_PAIR
    n_steps = npl + (npc if ctx_queries else 0)
    kvw = WIN_KV_HEADS * HEAD_DIM
    qw = WIN_HEADS * HEAD_DIM
    pair_rows = WIN_PAIR * BLOCK

    def q_blk(b, j):
        return jnp.where(j < npl, b * npl + j, B * npl + b * npc + (j - npl))

    def mid(b, j):
        return b * npl + jnp.minimum(j, npl - 1)

    def prev(b, j):
        return b * nqb + jnp.clip(j * WIN_PAIR - 1, 0, nqb - 1)

    def nxt(b, j):
        return b * nqb + jnp.clip(j * WIN_PAIR + WIN_PAIR, 0, nqb - 1)

    kspec = [pl.BlockSpec((BLOCK, kvw), lambda b, j: (prev(b, j), 0)),
             pl.BlockSpec((pair_rows, kvw), lambda b, j: (mid(b, j), 0)),
             pl.BlockSpec((BLOCK, kvw), lambda b, j: (nxt(b, j), 0)),
             pl.BlockSpec((Lc, kvw), lambda b, j: (B * S // Lc + b, 0))]
    vspec = [pl.BlockSpec((kvw, BLOCK), lambda b, j: (0, prev(b, j))),
             pl.BlockSpec((kvw, pair_rows), lambda b, j: (0, mid(b, j))),
             pl.BlockSpec((kvw, BLOCK), lambda b, j: (0, nxt(b, j))),
             pl.BlockSpec((kvw, Lc), lambda b, j: (0, B * S // Lc + b))]
    return pl.pallas_call(
        functools.partial(_window_kernel, dims),
        grid=(B, n_steps),
        in_specs=[pl.BlockSpec((qw, pair_rows), lambda b, j: (0, q_blk(b, j)))] + kspec + vspec
        + [pl.BlockSpec(memory_space=pltpu.SMEM)],
        out_specs=pl.BlockSpec((pair_rows, qw), lambda b, j: (q_blk(b, j), 0)),
        out_shape=jax.ShapeDtypeStruct((R, qw), BF16),
        compiler_params=_cparams("arbitrary", "arbitrary"),
        name="window_attn",
    )(bqT, bk, bk, bk, bk, bvT, bvT, bvT, bvT, sink)


GLB_TK = 128
GLB_ONLINE_TK = 256
GLB_MAX_SAFE_BOUND = 40.0


def _global_kernel(dims, par_ref, qT_ref, kl_ref, kx_ref, vTl_ref, vTx_ref, o_ref):
    B, S, Lc, R = dims
    nqb = S // BLOCK
    is_lat = pl.program_id(1) < nqb
    grp = GLB_HEADS // GLB_KV_HEADS
    cols = grp * BLOCK
    shift = par_ref[0]
    W = _stacked_q_weights(qT_ref[...], GLB_KV_HEADS, grp)

    def scores(kk):
        return tuple(jnp.dot(kk, W[kvh], preferred_element_type=F32) for kvh in range(GLB_KV_HEADS))

    def v_rows(vT, kvh):
        return vT[kvh * HEAD_DIM:(kvh + 1) * HEAD_DIM, :]

    def bounded(chunks):
        ls = [jnp.zeros((1, cols), F32)] * GLB_KV_HEADS
        accs = [jnp.zeros((HEAD_DIM, cols), F32)] * GLB_KV_HEADS
        s_cur = scores(chunks[0][0][chunks[0][2]:chunks[0][2] + GLB_TK, :])
        for i, (k_ref, v_ref, r0) in enumerate(chunks):
            s_next = None
            if i + 1 < len(chunks):
                nk, _, nr = chunks[i + 1]
                s_next = scores(nk[nr:nr + GLB_TK, :])
            vT = v_ref[:, r0:r0 + GLB_TK]
            for kvh in range(GLB_KV_HEADS):
                pT = jnp.exp2(s_cur[kvh] - shift)
                ls[kvh] = ls[kvh] + jnp.sum(pT, axis=0, keepdims=True)
                accs[kvh] = accs[kvh] + jnp.dot(v_rows(vT, kvh), pT.astype(BF16), preferred_element_type=F32)
            s_cur = s_next
        return tuple(acc * (1.0 / l) for l, acc in zip(ls, accs))

    def online():
        def update(carry, kk, vT):
            new = []
            for kvh, sT in enumerate(scores(kk)):
                m, l, acc = carry[kvh]
                m_new = jnp.maximum(m, jnp.max(sT, axis=0, keepdims=True))
                alpha = jnp.exp2(m - m_new)
                pT = jnp.exp2(sT - m_new)
                l = alpha * l + jnp.sum(pT, axis=0, keepdims=True)
                acc = alpha * acc + jnp.dot(v_rows(vT, kvh), pT.astype(BF16), preferred_element_type=F32)
                new.append((m_new, l, acc))
            return tuple(new)

        init = tuple((jnp.full((1, cols), NEG_INF, F32), jnp.zeros((1, cols), F32),
                      jnp.zeros((HEAD_DIM, cols), F32)) for _ in range(GLB_KV_HEADS))
        carry = init
        for r0 in range(0, Lc, GLB_ONLINE_TK):
            carry = update(carry, kx_ref[r0:r0 + GLB_ONLINE_TK, :], vTx_ref[:, r0:r0 + GLB_ONLINE_TK])

        def body(c, carry):
            r0 = pl.multiple_of(c * GLB_ONLINE_TK, GLB_ONLINE_TK)
            return update(carry, kl_ref[pl.ds(r0, GLB_ONLINE_TK), :], vTl_ref[:, pl.ds(r0, GLB_ONLINE_TK)])

        carry = lax.fori_loop(0, jnp.where(is_lat, S // GLB_ONLINE_TK, 0), body, carry)
        return tuple(acc * (1.0 / l) for (m, l, acc) in carry)

    ctx_chunks = [(kx_ref, vTx_ref, r0) for r0 in range(0, Lc, GLB_TK)]
    lat_chunks = [(kl_ref, vTl_ref, r0) for r0 in range(0, S, GLB_TK)]

    def bounded_any():
        return lax.cond(is_lat, lambda: bounded(ctx_chunks + lat_chunks), lambda: bounded(ctx_chunks))

    oT = lax.cond(par_ref[1] > 0.5, bounded_any, online)
    o_ref[...] = _unstack_heads(oT, GLB_KV_HEADS, grp).astype(o_ref.dtype)


def _global_call(dims, par, dqT, dk, dvT, ctx_queries):
    B, S, Lc, R = dims
    nqb, ncb = S // BLOCK, Lc // BLOCK
    n_steps = nqb + (ncb if ctx_queries else 0)
    lat_blocks = B * nqb
    kvw = GLB_KV_HEADS * HEAD_DIM
    qw = GLB_HEADS * HEAD_DIM
    assert Lc % GLB_ONLINE_TK == 0 and S % GLB_ONLINE_TK == 0 and GLB_ONLINE_TK % GLB_TK == 0

    def q_blk(b, j):
        return jnp.where(j < nqb, b * nqb + j, lat_blocks + b * ncb + (j - nqb))

    return pl.pallas_call(
        functools.partial(_global_kernel, dims),
        grid=(B, n_steps),
        in_specs=[pl.BlockSpec(memory_space=pltpu.SMEM),
                  pl.BlockSpec((qw, BLOCK), lambda b, j: (0, q_blk(b, j))),
                  pl.BlockSpec((S, kvw), lambda b, j: (b, 0)),
                  pl.BlockSpec((Lc, kvw), lambda b, j: (B * S // Lc + b, 0)),
                  pl.BlockSpec((kvw, S), lambda b, j: (0, b)),
                  pl.BlockSpec((kvw, Lc), lambda b, j: (0, B * S // Lc + b))],
        out_specs=pl.BlockSpec((BLOCK, qw), lambda b, j: (q_blk(b, j), 0)),
        out_shape=jax.ShapeDtypeStruct((R, qw), BF16),
        compiler_params=_cparams("arbitrary", "arbitrary"),
        name="global_attn",
    )(par, dqT, dk, dk, dvT, dvT)


RET_HPS = 2
RET_KERNEL_CHUNK = 256


def _ret_kernel(dims, ql_ref, qx_ref, kl_ref, kx_ref, vl_ref, vx_ref, gl_ref, gx_ref, lg_ref, gn_ref,
                ol_ref, ox_ref, fl_buf, fx_buf):
    B, S, Lc, R = dims
    C = RET_KERNEL_CHUNK
    assert S % C == 0 and Lc % C == 0
    hp = pl.program_id(1)
    pos_r = lax.broadcasted_iota(jnp.int32, (C, C), 0).astype(F32)
    pos_c = lax.broadcasted_iota(jnp.int32, (C, C), 1).astype(F32)
    diff = pos_r - pos_c
    col_pos = lax.broadcasted_iota(jnp.int32, (C, 1), 0).astype(F32)

    tabs = {}
    for dr in range(2):
        for hh in range(RET_HPS):
            lg = lg_ref[dr, hp * RET_HPS + hh]
            dd = diff if dr == 0 else -diff
            dmat = jnp.where(dd >= 0, jnp.exp(lg * jnp.maximum(dd, 0.0)), 0.0)
            p_eff = col_pos if dr == 0 else (C - 1.0) - col_pos
            kdec = jnp.exp(lg * ((C - 1.0) - p_eff))
            qdec = jnp.exp(lg * (p_eff + 1.0))
            cdec = jnp.exp(jnp.zeros((1, RET_V_DIM), F32) + lg * float(C))
            tabs[dr, hh] = (dmat, kdec, qdec, cdec)

    qk_w = RET_HPS * RET_QK_DIM
    lane = lax.broadcasted_iota(jnp.int32, (1, qk_w), 1)
    row = lax.broadcasted_iota(jnp.int32, (qk_w, 1), 0)
    lane_m = [jnp.where((lane >= hh * RET_QK_DIM) & (lane < (hh + 1) * RET_QK_DIM), 1.0, 0.0)
              for hh in range(RET_HPS)]
    row_m = [jnp.where((row >= hh * RET_QK_DIM) & (row < (hh + 1) * RET_QK_DIM), 1.0, 0.0)
             for hh in range(RET_HPS)]

    def step(q_ref, k_ref, v_ref, r0, dr, hh, state):
        dmat, kdec, qdec, cdec = tabs[dr, hh]
        vs = slice(hh * RET_V_DIM, (hh + 1) * RET_V_DIM)
        qf = q_ref[pl.ds(r0, C), :].astype(F32) * lane_m[hh]
        k2 = k_ref[pl.ds(r0, C), :]
        vc = v_ref[pl.ds(r0, C), vs]
        sc = lax.dot_general(qf.astype(BF16), k2, (((1,), (1,)), ((), ())), preferred_element_type=F32) * dmat
        inner = jnp.dot(sc.astype(BF16), vc, preferred_element_type=F32)
        qd = (qf * qdec).astype(BF16)
        cross = jnp.dot(qd, state.astype(BF16), preferred_element_type=F32)
        kd = (k2.astype(F32) * kdec).astype(BF16)
        u = lax.dot_general(kd, vc, (((0,), (0,)), ((), ())), preferred_element_type=F32)
        return inner + cross, cdec * state + u * row_m[hh]

    def sweep(q_ref, k_ref, v_ref, f_buf, n_chunks, states):
        def body(i, st):
            st = list(st)
            rf = pl.multiple_of(i * C, C)
            rb = pl.multiple_of((n_chunks - 1 - i) * C, C)
            for hh in range(RET_HPS):
                vs = slice(hh * RET_V_DIM, (hh + 1) * RET_V_DIM)
                of, st[hh] = step(q_ref, k_ref, v_ref, rf, 0, hh, st[hh])
                f_buf[0, pl.ds(rf, C), vs] = of
                ob, st[RET_HPS + hh] = step(q_ref, k_ref, v_ref, rb, 1, hh, st[RET_HPS + hh])
                f_buf[1, pl.ds(rb, C), vs] = ob
            return tuple(st)
        return lax.fori_loop(0, n_chunks, body, states, unroll=4 if n_chunks % 4 == 0 else 1)

    def finish(f_buf, g_ref, o_ref, n_chunks):
        def body(i, carry):
            r0 = pl.multiple_of(i * C, C)
            for hh in range(RET_HPS):
                vs = slice(hh * RET_V_DIM, (hh + 1) * RET_V_DIM)
                o = f_buf[0, pl.ds(r0, C), vs] + f_buf[1, pl.ds(r0, C), vs]
                mu = jnp.mean(o, axis=-1, keepdims=True)
                xc = o - mu
                var = jnp.mean(xc * xc, axis=-1, keepdims=True)
                y = xc * lax.rsqrt(var + EPS) * gn_ref[:, vs]
                o_ref[pl.ds(r0, C), vs] = (_silu(g_ref[pl.ds(r0, C), vs]) * y).astype(o_ref.dtype)
            return carry
        lax.fori_loop(0, n_chunks, body, 0)

    zero = jnp.zeros((qk_w, RET_V_DIM), F32)
    states = sweep(qx_ref, kx_ref, vx_ref, fx_buf, Lc // C, (zero,) * (2 * RET_HPS))
    sweep(ql_ref, kl_ref, vl_ref, fl_buf, S // C, states)
    finish(fx_buf, gx_ref, ox_ref, Lc // C)
    finish(fl_buf, gl_ref, ol_ref, S // C)


def _ret_call(dims, cq, ck, cv, cg, log_gamma, gn_g):
    B, S, Lc, R = dims
    qw = RET_HPS * RET_QK_DIM
    vw = RET_HPS * RET_V_DIM
    nhp = RET_HEADS // RET_HPS
    xoff = B * S // Lc

    def lat(w):
        return pl.BlockSpec((S, w), lambda b, h: (b, h))

    def ctx(w):
        return pl.BlockSpec((Lc, w), lambda b, h: (xoff + b, h))

    return pl.pallas_call(
        functools.partial(_ret_kernel, dims),
        grid=(B, nhp),
        in_specs=[lat(qw), ctx(qw), lat(qw), ctx(qw), lat(vw), ctx(vw), lat(vw), ctx(vw),
                  pl.BlockSpec(memory_space=pltpu.SMEM),
                  pl.BlockSpec((1, vw), lambda b, h: (0, h))],
        out_specs=[pl.BlockSpec((S, vw), lambda b, h: (b, h)),
                   pl.BlockSpec((Lc, vw), lambda b, h: (b, h))],
        out_shape=[jax.ShapeDtypeStruct((B * S, RET_HEADS * RET_V_DIM), BF16),
                   jax.ShapeDtypeStruct((B * Lc, RET_HEADS * RET_V_DIM), BF16)],
        scratch_shapes=[pltpu.VMEM((2, S, vw), F32), pltpu.VMEM((2, Lc, vw), F32)],
        compiler_params=_cparams("arbitrary", "arbitrary"),
        name="retention",
    )(cq, cq, ck, ck, cv, cv, cg, cg, log_gamma, gn_g.reshape(1, -1))


MERGE_SUB = 256


def _merge_kernel(n_lat_tiles, ya_ref, yb_ref, yrl_ref, yrx_ref, yd_ref, gt_ref, x_ref, mod_ref, g_ref,
                  wbr_ref, wo_ref, o_ref):
    tm = x_ref.shape[0]
    is_lat = pl.program_id(0) < n_lat_tiles
    gate = mod_ref[0][:, 2 * D_MODEL:3 * D_MODEL]
    subs = [slice(r0, r0 + MERGE_SUB) for r0 in range(0, tm, MERGE_SUB)]

    def branch_dots(rs):
        yr = jnp.where(is_lat, yrl_ref[rs, :], yrx_ref[rs, :])
        return [jnp.dot(b, wbr_ref[i], preferred_element_type=F32)
                for i, b in enumerate((ya_ref[rs, :], yb_ref[rs, :], yr, yd_ref[rs, :]))]

    dots = [branch_dots(rs) for rs in subs]
    for rs, d in zip(subs, dots):
        acc = None
        for i in range(N_BRANCH):
            t = gt_ref[rs, i * D_MODEL:(i + 1) * D_MODEL].astype(F32) * d[i]
            acc = t if acc is None else acc + t
        y = jnp.dot(acc.astype(BF16), wo_ref[...], preferred_element_type=F32)
        r = y * lax.rsqrt(jnp.mean(y * y, axis=-1, keepdims=True) + EPS) * g_ref[...]
        o_ref[rs, :] = x_ref[rs, :] + gate * r


def _merge_call(dims, ya, yb, yr_l, yr_x, yd, gates, xall, mod, g, w_br, w_o, tm, rows_out):
    B, S, Lc, R = dims
    assert tm % MERGE_SUB == 0
    n_lat = B * S // tm
    n_ctx = B * Lc // tm

    def row(i):
        return (i, 0)

    br = pl.BlockSpec((tm, BRANCH_DIM), row)
    return pl.pallas_call(
        functools.partial(_merge_kernel, n_lat),
        grid=(rows_out // tm,),
        in_specs=[br, br,
                  pl.BlockSpec((tm, BRANCH_DIM), lambda i: (jnp.minimum(i, n_lat - 1), 0)),
                  pl.BlockSpec((tm, BRANCH_DIM), lambda i: (jnp.clip(i - n_lat, 0, n_ctx - 1), 0)),
                  br,
                  pl.BlockSpec((tm, N_BRANCH * D_MODEL), row),
                  pl.BlockSpec((tm, D_MODEL), row),
                  pl.BlockSpec((1, 1, 6 * D_MODEL), lambda i: (jnp.minimum(i * tm // S, B), 0, 0)),
                  pl.BlockSpec((1, D_MODEL), lambda i: (0, 0)),
                  _resident((N_BRANCH, BRANCH_DIM, D_MODEL)),
                  _resident((D_MODEL, D_MODEL))],
        out_specs=pl.BlockSpec((tm, D_MODEL), row),
        out_shape=jax.ShapeDtypeStruct((rows_out, D_MODEL), F32),
        compiler_params=_cparams("arbitrary"),
        name="merge",
    )(ya, yb, yr_l, yr_x, yd, gates, xall, mod, g, w_br, w_o)


FFN_HALO = 8
FFN_CN = 256
FFN_LOOKAHEAD = 3


def _ffn_kernel(dims, tm, x_ref, xp_ref, xn_ref, mod_ref, g2_ref, g3_ref, wup_ref, dw_ref, db_ref, wdn_ref,
                o_ref, acc_buf):
    B, S, Lc, R = dims
    first, last = _seq_edges(pl.program_id(0), tm, B * S, S, Lc)
    m = mod_ref[0]
    g2 = g2_ref[...]
    x = x_ref[...]
    hp = jnp.where(first, 0.0, _modulated(xp_ref[...], g2, m, 1))
    hn = jnp.where(last, 0.0, _modulated(xn_ref[...], g2, m, 1))
    h = jnp.concatenate([hp, _modulated(x, g2, m, 1), hn], axis=0).astype(BF16)

    rows = tm + 2 * FFN_HALO

    def up(j):
        return tuple(jnp.dot(h, wup_ref[:, c0:c0 + FFN_CN], preferred_element_type=F32)
                     for c0 in (j * FFN_CN, D_FF + j * FFN_CN))

    def conv3(u, col0):
        w = dw_ref[:, col0:col0 + FFN_CN]
        y = (w[0:1] * pltpu.roll(u, 1, 0) + w[1:2] * u + w[2:3] * pltpu.roll(u, rows - 1, 0)
             + db_ref[:, col0:col0 + FFN_CN])
        return y[FFN_HALO:FFN_HALO + tm]

    n_chunks = D_FF // FFN_CN
    ahead = [up(j) for j in range(FFN_LOOKAHEAD)]
    for j in range(n_chunks):
        ca, cb = j * FFN_CN, D_FF + j * FFN_CN
        if j + FFN_LOOKAHEAD < n_chunks:
            ahead.append(up(j + FFN_LOOKAHEAD))
        u_cur = ahead.pop(0)
        a, b = conv3(u_cur[0], ca), conv3(u_cur[1], cb)
        t = (_silu(a) * b).astype(BF16)
        part = jnp.dot(t, wdn_ref[ca:ca + FFN_CN, :], preferred_element_type=F32)
        if j == 0:
            acc_buf[...] = part
        else:
            acc_buf[...] += part

    y = acc_buf[...]
    r = y * lax.rsqrt(jnp.mean(y * y, axis=-1, keepdims=True) + EPS) * g3_ref[...]
    o_ref[...] = x + m[:, 5 * D_MODEL:6 * D_MODEL] * r


def _ffn_call(dims, xall, mod, g2, g3, f_up, f_dw, f_dw_b, f_down, tm):
    B, S, Lc, R = dims
    rows = xall.shape[0]
    assert rows in (R, B * S) and S % tm == 0 and Lc % tm == 0
    hb = tm // FFN_HALO
    nhb = rows // FFN_HALO
    return pl.pallas_call(
        functools.partial(_ffn_kernel, dims, tm),
        grid=(rows // tm,),
        in_specs=[
            pl.BlockSpec((tm, D_MODEL), lambda i: (i, 0)),
            pl.BlockSpec((FFN_HALO, D_MODEL), lambda i: (jnp.maximum(i * hb - 1, 0), 0)),
            pl.BlockSpec((FFN_HALO, D_MODEL), lambda i: (jnp.minimum((i + 1) * hb, nhb - 1), 0)),
            pl.BlockSpec((1, 1, 6 * D_MODEL), lambda i: (jnp.minimum(i * tm // S, B), 0, 0)),
            pl.BlockSpec((1, D_MODEL), lambda i: (0, 0)),
            pl.BlockSpec((1, D_MODEL), lambda i: (0, 0)),
            _resident((D_MODEL, 2 * D_FF)),
            pl.BlockSpec((FFN_CONV_WIDTH, 2 * D_FF), lambda i: (0, 0)),
            pl.BlockSpec((1, 2 * D_FF), lambda i: (0, 0)),
            _resident((D_FF, D_MODEL)),
        ],
        out_specs=pl.BlockSpec((tm, D_MODEL), lambda i: (i, 0)),
        out_shape=jax.ShapeDtypeStruct((rows, D_MODEL), F32),
        scratch_shapes=[pltpu.VMEM((tm, D_MODEL), F32)],
        compiler_params=_cparams("arbitrary"),
        name="conv_ffn",
    )(xall, xall, xall, mod, g2, g3, f_up, f_dw, f_dw_b.reshape(1, -1), f_down)


def _rope_tables(S, pad_rows):
    rows = S // GRID_W
    row = jnp.repeat(jnp.arange(rows), GRID_W).astype(F32)
    col = jnp.tile(jnp.arange(GRID_W), rows).astype(F32)
    half = HEAD_DIM // 2
    inv = ROPE_BASE ** (-jnp.arange(0, half, 2, dtype=F32) / half)
    ang = jnp.concatenate([row[:, None] * inv, col[:, None] * inv], axis=-1)
    cos, sin = jnp.cos(ang), jnp.sin(ang)
    zero = jnp.zeros_like(sin)
    cos_h = jnp.concatenate([cos, cos], axis=-1)
    sa_h = jnp.concatenate([-sin, zero], axis=-1)
    sb_h = jnp.concatenate([zero, sin], axis=-1)
    reps = LANES // HEAD_DIM

    def slab(t, fill):
        t = jnp.tile(t, (1, reps))
        return jnp.concatenate([t, jnp.full((pad_rows, LANES), fill, F32)], axis=0)

    return slab(cos_h, 1.0), slab(sa_h, 0.0), slab(sb_h, 0.0)


def kernel(x, c, ctx, c_ctx, ada_w, ada_b, norm_g, w_in, a_dw, a_dw_b, a_ln_g, a_ln_b, b_sink,
           c_decay_logit, c_gn_g, d_qn_g, d_kn_g, w_br, w_o, f_up, f_dw, f_dw_b, f_down):
    B, S, _ = x.shape
    Lc = ctx.shape[1]
    R = B * (S + Lc)
    dims = (B, S, Lc, R)
    tm = ROW_TILE
    assert B + 1 <= MOD_ROWS and S % tm == 0 and (B * Lc) % tm == 0 and Lc % BLOCK == 0 and S % Lc == 0

    cvec = jnp.concatenate([c, c_ctx[None, :], jnp.zeros((MOD_ROWS - B - 1, D_MODEL), F32)], axis=0)
    mods = _ada_call(cvec, ada_w, ada_b).reshape(DEPTH, MOD_ROWS, 1, 6 * D_MODEL)
    cos_t, sa_t, sb_t = _rope_tables(S, tm)
    reps = LANES // HEAD_DIM
    log_gamma = jnp.log(jax.nn.sigmoid(c_decay_logit.astype(F32)))

    xall = jnp.concatenate([x.reshape(B * S, D_MODEL), ctx.reshape(B * Lc, D_MODEL)], axis=0)
    for l in range(DEPTH):
        mod = mods[l]
        ng = norm_g[l].reshape(4, 1, D_MODEL)
        (z, bq, bk, bv, cq, ck, cv, cg, dq, dk, dv, gates) = _proj_call(
            dims, xall, mod, ng[0], w_in[l].astype(BF16), cos_t, sa_t, sb_t,
            jnp.tile(d_qn_g[l], reps)[None, :], jnp.tile(d_kn_g[l], reps)[None, :], tm)
        with_ctx = l < DEPTH - 1
        ya = _conv_call(dims, z, a_dw[l], a_dw_b[l], a_ln_g[l], a_ln_b[l], 256)
        yb = _window_call(dims, bq, bk, bv, b_sink[l], with_ctx)
        yr_l, yr_x = _ret_call(dims, cq, ck, cv, cg, log_gamma[l], c_gn_g[l])
        bound = 1.02 * HEAD_DIM * (HEAD_DIM ** -0.5) * jnp.max(jnp.abs(d_qn_g[l])) * jnp.max(jnp.abs(d_kn_g[l]))
        par = jnp.stack([bound * LOG2E, (bound <= GLB_MAX_SAFE_BOUND).astype(F32)])
        yd = _global_call(dims, par, dq, dk, dv, with_ctx)
        rows_out = R if with_ctx else B * S
        xall = _merge_call(dims, ya, yb, yr_l, yr_x, yd, gates, xall, mod, ng[1],
                           w_br[l].astype(BF16), w_o[l].astype(BF16), 2 * MERGE_SUB, rows_out)
        xall = _ffn_call(dims, xall, mod, ng[2], ng[3], f_up[l].astype(BF16), f_dw[l], f_dw_b[l],
                         f_down[l].astype(BF16), tm)
    return xall.reshape(B, S, D_MODEL)
```
